```python
import math
import jax, jax.numpy as jnp
from jax import lax
import numpy as np

D_MODEL = 2048
BATCH = 2
SEQ = 4096
DEPTH = 4

N_MIXERS = 2
N_POOL_LAYERS = (DEPTH + 1) // 2
N_FOX_LAYERS = DEPTH // 2

POOL_WINDOWS = (2, 4, 8, 16)
N_POOL_GROUPS = len(POOL_WINDOWS)
POOL_GROUP = D_MODEL // N_POOL_GROUPS

FOX_HEAD_DIM = 128
FOX_HEADS = D_MODEL // FOX_HEAD_DIM
Q_BLOCK = 128

D_FF = 5632
FFN_RESIDUAL_WEIGHT = 0.5
RMS_EPS = 1e-6
NEG_LARGE = -1e30

kernel_name = "hybrid_pool_fox_macaron"


def rms_norm(x, g):
    xf = x.astype(jnp.float32)
    y = xf * lax.rsqrt(jnp.mean(xf * xf, axis=-1, keepdims=True) + RMS_EPS)
    return (y * g.astype(jnp.float32)).astype(x.dtype)


def swiglu(x, w_gate, w_up, w_down):
    return (jax.nn.silu(x @ w_gate) * (x @ w_up)) @ w_down


def causal_trailing_mean(xg, window):
    T = xg.shape[1]
    c = jnp.pad(jnp.cumsum(xg.astype(jnp.float32), axis=1), ((0, 0), (1, 0), (0, 0)))
    hi = c[:, 1:]
    lo = jnp.pad(c[:, :T + 1 - window], ((0, 0), (window - 1, 0), (0, 0)))
    count = jnp.minimum(jnp.arange(1, T + 1), window).astype(jnp.float32)[None, :, None]
    return (hi - lo) / count


def pool_mixer(h, w_groups, scale):
    B, T, D = h.shape
    hg = h.reshape(B, T, N_POOL_GROUPS, POOL_GROUP)
    pooled = jnp.stack([causal_trailing_mean(hg[:, :, g], w) for g, w in enumerate(POOL_WINDOWS)], axis=2)
    diff = (pooled - hg.astype(jnp.float32)).astype(h.dtype)
    y = jnp.einsum('btgc,gcd->btgd', diff, w_groups).reshape(B, T, D)
    return y * scale


def fox_mixer(h, w_in, b_f, q_gain, k_gain, w_out):
    B, T, D = h.shape
    proj = h @ w_in
    q = proj[..., :D].reshape(B, T, FOX_HEADS, FOX_HEAD_DIM)
    k = proj[..., D:2 * D].reshape(B, T, FOX_HEADS, FOX_HEAD_DIM)
    v = proj[..., 2 * D:3 * D].reshape(B, T, FOX_HEADS, FOX_HEAD_DIM)
    f_logit = proj[..., 3 * D:]
    q = rms_norm(q, q_gain) * (1.0 / math.sqrt(FOX_HEAD_DIM))
    k = rms_norm(k, k_gain)
    log_f = jax.nn.log_sigmoid((f_logit + b_f).astype(jnp.float32))
    F = jnp.cumsum(log_f, axis=1).transpose(0, 2, 1)
    q = q.transpose(0, 2, 1, 3)
    k = k.transpose(0, 2, 1, 3)
    v = v.transpose(0, 2, 1, 3)
    nb = T // Q_BLOCK
    q_blocks = q.reshape(B, FOX_HEADS, nb, Q_BLOCK, FOX_HEAD_DIM).transpose(2, 0, 1, 3, 4)
    F_blocks = F.reshape(B, FOX_HEADS, nb, Q_BLOCK).transpose(2, 0, 1, 3)
    starts = jnp.arange(nb, dtype=jnp.int32) * Q_BLOCK
    kpos = jnp.arange(T, dtype=jnp.int32)

    def attend_block(args):
        qi, Fi, s0 = args
        logits = jnp.einsum('bhqd,bhkd->bhqk', qi, k).astype(jnp.float32)
        logits = logits + Fi[..., :, None] - F[..., None, :]
        qpos = s0 + jnp.arange(Q_BLOCK, dtype=jnp.int32)
        mask = kpos[None, :] <= qpos[:, None]
        logits = jnp.where(mask, logits, NEG_LARGE)
        p = jax.nn.softmax(logits, axis=-1).astype(v.dtype)
        return jnp.einsum('bhqk,bhkd->bhqd', p, v)

    o = lax.map(attend_block, (q_blocks, F_blocks, starts))
    o = o.transpose(1, 0, 3, 2, 4).reshape(B, T, D)
    return o @ w_out


def setup_inputs(seed: int = 0) -> dict:
    key = jax.random.key(seed)
    ks = iter(jax.random.split(key, 32))
    nrm = lambda shape, s: jax.random.normal(next(ks), shape, jnp.float32) * s
    gain = lambda shape: 1.0 + nrm(shape, 0.02)
    D, F, H = D_MODEL, D_FF, FOX_HEADS
    return {
        "x": nrm((BATCH, SEQ, D), 1.0),
        "ffn1_norm": gain((DEPTH, D)),
        "ffn1_w_gate": nrm((DEPTH, D, F), D ** -0.5),
        "ffn1_w_up": nrm((DEPTH, D, F), D ** -0.5),
        "ffn1_w_down": nrm((DEPTH, F, D), F ** -0.5),
        "mix_norm": gain((DEPTH, D)),
        "pool_w": nrm((N_POOL_LAYERS, N_POOL_GROUPS, POOL_GROUP, POOL_GROUP), POOL_GROUP ** -0.5),
        "pool_scale": 1.0 + nrm((N_POOL_LAYERS, D), 0.1),
        "fox_w_in": nrm((N_FOX_LAYERS, D, 3 * D + H), D ** -0.5),
        "fox_b_f": nrm((N_FOX_LAYERS, H), 0.1),
        "fox_q_gain": gain((N_FOX_LAYERS, FOX_HEAD_DIM)),
        "fox_k_gain": gain((N_FOX_LAYERS, FOX_HEAD_DIM)),
        "fox_w_out": nrm((N_FOX_LAYERS, D, D), D ** -0.5),
        "ffn2_norm": gain((DEPTH, D)),
        "ffn2_w_gate": nrm((DEPTH, D, F), D ** -0.5),
        "ffn2_w_up": nrm((DEPTH, D, F), D ** -0.5),
        "ffn2_w_down": nrm((DEPTH, F, D), F ** -0.5),
    }


def reference(x, ffn1_norm, ffn1_w_gate, ffn1_w_up, ffn1_w_down, mix_norm, pool_w, pool_scale,
              fox_w_in, fox_b_f, fox_q_gain, fox_k_gain, fox_w_out,
              ffn2_norm, ffn2_w_gate, ffn2_w_up, ffn2_w_down):
    for i in range(DEPTH):
        x = x + FFN_RESIDUAL_WEIGHT * swiglu(rms_norm(x, ffn1_norm[i]), ffn1_w_gate[i], ffn1_w_up[i], ffn1_w_down[i])
        h = rms_norm(x, mix_norm[i])
        j = i // N_MIXERS
        if i % N_MIXERS == 0:
            x = x + pool_mixer(h, pool_w[j], pool_scale[j])
        else:
            x = x + fox_mixer(h, fox_w_in[j], fox_b_f[j], fox_q_gain[j], fox_k_gain[j], fox_w_out[j])
        x = x + FFN_RESIDUAL_WEIGHT * swiglu(rms_norm(x, ffn2_norm[i]), ffn2_w_gate[i], ffn2_w_up[i], ffn2_w_down[i])
    return x
```

```python
import functools
import math

import jax
import jax.numpy as jnp
from jax import lax
from jax.experimental import pallas as pl
from jax.experimental.pallas import tpu as pltpu

RMS_EPS = 1e-6
FFN_RESIDUAL_WEIGHT = 0.5
POOL_WINDOWS = (2, 4, 8, 16)
FOX_HEAD_DIM = 128
NEG_LARGE = -1e30

V7X_LANES = 128
V7X_VMEM_BYTES = 64 * 1024 * 1024
POOL_HALO = 32
POOL_FIRST_ROW = 8

BF16 = jnp.bfloat16
F32 = jnp.float32


def _vmem_limit(block_bytes, temp_bytes):
    return min(2 * block_bytes + temp_bytes + (8 << 20), V7X_VMEM_BYTES - (6 << 20))


def _rms_norm(x, gain):
    ms = jnp.mean(x * x, axis=-1, keepdims=True)
    return x * lax.rsqrt(ms + RMS_EPS) * gain


def _ffn_kernel(x_ref, g_ref, wg_ref, wu_ref, wd_ref, o_ref, h_ref):
    j = pl.program_id(1)

    @pl.when(j == 0)
    def _():
        h_ref[...] = _rms_norm(x_ref[...], g_ref[...]).astype(BF16)
        o_ref[...] = jnp.zeros_like(o_ref)

    h = h_ref[...]
    gate = jnp.dot(h, wg_ref[...], preferred_element_type=F32)
    up = jnp.dot(h, wu_ref[...], preferred_element_type=F32)
    act = (gate * jax.nn.sigmoid(gate) * up).astype(BF16)
    o_ref[...] += jnp.dot(act, wd_ref[...], preferred_element_type=F32)

    @pl.when(j == pl.num_programs(1) - 1)
    def _():
        o_ref[...] = x_ref[...] + FFN_RESIDUAL_WEIGHT * o_ref[...]


def _ffn(x, gain, w_gate, w_up, w_down, *, tm=512, tf=512):
    n, d = x.shape
    f = w_gate.shape[1]
    assert n % tm == 0 and f % tf == 0
    blocks = 2 * tm * d * 4 + 2 * d * tf * 2 + tf * d * 2 + d * 4
    temps = tm * d * 2 + 4 * tm * tf * 4
    return pl.pallas_call(
        _ffn_kernel,
        out_shape=jax.ShapeDtypeStruct((n, d), F32),
        grid=(n // tm, f // tf),
        in_specs=[
            pl.BlockSpec((tm, d), lambda i, j: (i, 0)),
            pl.BlockSpec((1, d), lambda i, j: (0, 0)),
            pl.BlockSpec((d, tf), lambda i, j: (0, j)),
            pl.BlockSpec((d, tf), lambda i, j: (0, j)),
            pl.BlockSpec((tf, d), lambda i, j: (j, 0)),
        ],
        out_specs=pl.BlockSpec((tm, d), lambda i, j: (i, 0)),
        scratch_shapes=[pltpu.VMEM((tm, d), BF16)],
        compiler_params=pltpu.CompilerParams(
            dimension_semantics=("arbitrary", "arbitrary"),
            vmem_limit_bytes=_vmem_limit(blocks, temps)),
        name="ffn",
    )(x, gain.reshape(1, d), w_gate, w_up, w_down)


def _pool_kernel(x_ref, xh_ref, g_ref, w_ref, s_ref, o_ref, h_ref, p_ref, q_ref, *, tiles_per_batch):
    tm, d = x_ref.shape
    group = d // len(POOL_WINDOWS)
    rows = tm + POOL_HALO
    tile_in_batch = lax.rem(pl.program_id(0), tiles_per_batch)

    gain = g_ref[...]
    halo = _rms_norm(xh_ref[...], gain)
    h_ref[0:POOL_HALO, :] = jnp.where(tile_in_batch == 0, 0.0, halo)
    h_ref[POOL_HALO:, :] = _rms_norm(x_ref[...], gain)

    src = h_ref
    first_exact_row = POOL_FIRST_ROW
    for step in range(len(POOL_WINDOWS)):
        shift = 1 << step
        assert POOL_WINDOWS[step] == 2 * shift and shift <= POOL_FIRST_ROW
        first_exact_row += shift if step else 0
        dst = p_ref if step % 2 == 0 else q_ref
        c0 = step * group
        dst[POOL_FIRST_ROW:, c0:] = (src[POOL_FIRST_ROW:, c0:]
                                     + src[POOL_FIRST_ROW - shift:rows - shift, c0:])
        src = dst
    assert first_exact_row <= POOL_HALO

    pos = tile_in_batch * tm + lax.broadcasted_iota(jnp.int32, (tm, 1), 0)
    for gi, window in enumerate(POOL_WINDOWS):
        cols = slice(gi * group, (gi + 1) * group)
        sums = (p_ref if gi % 2 == 0 else q_ref)[POOL_HALO:, cols]
        count = jnp.minimum(pos + 1, window).astype(F32)
        diff = (sums / count - h_ref[POOL_HALO:, cols]).astype(BF16)
        y = jnp.dot(diff, w_ref[gi], preferred_element_type=F32)
        o_ref[:, cols] = x_ref[:, cols] + y * s_ref[:, cols]


def _pool(x, gain, w_groups, scale, *, seq, tm=512):
    n, d = x.shape
    groups, gsz, _ = w_groups.shape
    assert seq % tm == 0 and tm % POOL_HALO == 0 and groups * gsz == d
    halo_blocks = tm // POOL_HALO
    blocks = 2 * tm * d * 4 + POOL_HALO * d * 4 + groups * gsz * gsz * 2 + 2 * d * 4
    temps = 3 * (tm + POOL_HALO) * d * 4 + 2 * tm * d * 4
    return pl.pallas_call(
        functools.partial(_pool_kernel, tiles_per_batch=seq // tm),
        out_shape=jax.ShapeDtypeStruct((n, d), F32),
        grid=(n // tm,),
        in_specs=[
            pl.BlockSpec((tm, d), lambda i: (i, 0)),
            pl.BlockSpec((POOL_HALO, d), lambda i: (jnp.maximum(i * halo_blocks - 1, 0), 0)),
            pl.BlockSpec((1, d), lambda i: (0, 0)),
            pl.BlockSpec((groups, gsz, gsz), lambda i: (0, 0, 0)),
            pl.BlockSpec((1, d), lambda i: (0, 0)),
        ],
        out_specs=pl.BlockSpec((tm, d), lambda i: (i, 0)),
        scratch_shapes=[pltpu.VMEM((tm + POOL_HALO, d), F32)] * 3,
        compiler_params=pltpu.CompilerParams(
            dimension_semantics=("arbitrary",),
            vmem_limit_bytes=_vmem_limit(blocks, temps)),
        name="pool",
    )(x, x, gain.reshape(1, d), w_groups, scale.reshape(1, d))


def _log_sigmoid(z):
    return jnp.minimum(z, 0.0) - jnp.log1p(jnp.exp(-jnp.abs(z)))


def _fox_proj_kernel(x_ref, g_ref, w_ref, wf_ref, bf_ref, gq_ref, gk_ref, qkv_ref, f_ref, h_ref, carry_ref,
                     *, tiles_per_batch, d_model):
    i = pl.program_id(0)
    j = pl.program_id(1)
    tm, tn = qkv_ref.shape
    col_tiles = d_model // tn

    @pl.when(j == 0)
    def _():
        h = _rms_norm(x_ref[...], g_ref[...]).astype(BF16)
        h_ref[...] = h
        logit = jnp.dot(h, wf_ref[...], preferred_element_type=F32) + bf_ref[...]
        log_f = _log_sigmoid(logit)
        r = lax.broadcasted_iota(jnp.int32, (tm, tm), 0)
        c = lax.broadcasted_iota(jnp.int32, (tm, tm), 1)
        tri = (c <= r).astype(BF16)
        hi = log_f.astype(BF16)
        rest = log_f - hi.astype(F32)
        mid = rest.astype(BF16)
        lo = (rest - mid.astype(F32)).astype(BF16)
        prefix = (jnp.dot(tri, hi, preferred_element_type=F32)
                  + jnp.dot(tri, mid, preferred_element_type=F32)
                  + jnp.dot(tri, lo, preferred_element_type=F32))

        @pl.when(lax.rem(i, tiles_per_batch) == 0)
        def _():
            carry_ref[...] = jnp.zeros_like(carry_ref)

        total = prefix + carry_ref[...]
        f_ref[...] = total
        carry_ref[...] = total[tm - 1:tm, :]

    acc = jnp.dot(h_ref[...], w_ref[...], preferred_element_type=F32)

    def head_norm(gain_ref, post_scale):
        for c0 in range(0, tn, FOX_HEAD_DIM):
            y = _rms_norm(acc[:, c0:c0 + FOX_HEAD_DIM], gain_ref[...])
            if post_scale is not None:
                y = y * post_scale
            qkv_ref[:, c0:c0 + FOX_HEAD_DIM] = y.astype(BF16)

    @pl.when(j < col_tiles)
    def _():
        head_norm(gq_ref, 1.0 / math.sqrt(FOX_HEAD_DIM))

    @pl.when((j >= col_tiles) & (j < 2 * col_tiles))
    def _():
        head_norm(gk_ref, None)

    @pl.when(j >= 2 * col_tiles)
    def _():
        qkv_ref[...] = acc.astype(BF16)


def _fox_proj(x, gain, w_qkv, w_f, b_f, q_gain, k_gain, *, seq, tm=512, tn=512):
    n, d = x.shape
    n_cols = w_qkv.shape[1]
    assert seq % tm == 0 and d % tn == 0 and n_cols == 3 * d and tn % FOX_HEAD_DIM == 0
    blocks = tm * d * 4 + d * tn * 2 + d * V7X_LANES * 2 + tm * tn * 2 + tm * V7X_LANES * 4
    temps = tm * d * 2 + 3 * tm * tm * 4 + 3 * tm * tn * 4
    return pl.pallas_call(
        functools.partial(_fox_proj_kernel, tiles_per_batch=seq // tm, d_model=d),
        out_shape=(jax.ShapeDtypeStruct((n, n_cols), BF16), jax.ShapeDtypeStruct((n, V7X_LANES), F32)),
        grid=(n // tm, n_cols // tn),
        in_specs=[
            pl.BlockSpec((tm, d), lambda i, j: (i, 0)),
            pl.BlockSpec((1, d), lambda i, j: (0, 0)),
            pl.BlockSpec((d, tn), lambda i, j: (0, j)),
            pl.BlockSpec((d, V7X_LANES), lambda i, j: (0, 0)),
            pl.BlockSpec((1, V7X_LANES), lambda i, j: (0, 0)),
            pl.BlockSpec((1, FOX_HEAD_DIM), lambda i, j: (0, 0)),
            pl.BlockSpec((1, FOX_HEAD_DIM), lambda i, j: (0, 0)),
        ],
        out_specs=(pl.BlockSpec((tm, tn), lambda i, j: (i, j)),
                   pl.BlockSpec((tm, V7X_LANES), lambda i, j: (i, 0))),
        scratch_shapes=[pltpu.VMEM((tm, d), BF16), pltpu.VMEM((1, V7X_LANES), F32)],
        compiler_params=pltpu.CompilerParams(
            dimension_semantics=("arbitrary", "arbitrary"),
            vmem_limit_bytes=_vmem_limit(blocks, temps)),
        name="fox_proj",
    )(x, gain.reshape(1, d), w_qkv, w_f, b_f, q_gain.reshape(1, -1), k_gain.reshape(1, -1))


def _fox_attn_kernel(q_ref, k_ref, v_ref, fq_ref, fk_ref, o_ref, m_ref, l_ref, acc_ref, *, tk):
    qi = pl.program_id(2)
    tq = q_ref.shape[0]
    assert tq == tk
    q = q_ref[...]
    fq = fq_ref[...]

    m_ref[...] = jnp.full_like(m_ref, NEG_LARGE)
    l_ref[...] = jnp.zeros_like(l_ref)
    acc_ref[...] = jnp.zeros_like(acc_ref)

    def block(kb, masked):
        start = pl.multiple_of(kb * tk, tk)
        k = k_ref[pl.ds(start, tk), :]
        v = v_ref[pl.ds(start, tk), :]
        s = lax.dot_general(q, k, (((1,), (1,)), ((), ())), preferred_element_type=F32)
        s = s + fq - fk_ref[kb]
        if masked:
            r = lax.broadcasted_iota(jnp.int32, (tq, tk), 0)
            c = lax.broadcasted_iota(jnp.int32, (tq, tk), 1)
            s = jnp.where(c <= r, s, NEG_LARGE)
        m_old = m_ref[...]
        m_new = jnp.maximum(m_old, jnp.max(s, axis=-1, keepdims=True))
        alpha = jnp.exp(m_old - m_new)
        p = jnp.exp(s - m_new)
        l_ref[...] = alpha * l_ref[...] + jnp.sum(p, axis=-1, keepdims=True)
        acc_ref[...] = alpha * acc_ref[...] + jnp.dot(p.astype(BF16), v, preferred_element_type=F32)
        m_ref[...] = m_new

    def body(kb, carry):
        block(kb, False)
        return carry

    lax.fori_loop(0, qi, body, 0)
    block(qi, True)
    o_ref[...] = (acc_ref[...] / l_ref[...]).astype(BF16)


def _fox_attn(qkv, f_q, f_k, *, batch, seq, heads, tq=512):
    n = qkv.shape[0]
    d = heads * FOX_HEAD_DIM
    nq = seq // tq
    assert seq % tq == 0
    blocks = tq * FOX_HEAD_DIM * 2 + 2 * seq * FOX_HEAD_DIM * 2 + tq * V7X_LANES * 4 + 8 * seq * 4 \
        + tq * FOX_HEAD_DIM * 2
    temps = 3 * tq * V7X_LANES * 4 + 6 * tq * tq * 4
    return pl.pallas_call(
        functools.partial(_fox_attn_kernel, tk=tq),
        out_shape=jax.ShapeDtypeStruct((n, d), BF16),
        grid=(batch, heads, nq),
        in_specs=[
            pl.BlockSpec((tq, FOX_HEAD_DIM), lambda b, h, qi: (b * nq + qi, h)),
            pl.BlockSpec((seq, FOX_HEAD_DIM), lambda b, h, qi: (b, heads + h)),
            pl.BlockSpec((seq, FOX_HEAD_DIM), lambda b, h, qi: (b, 2 * heads + h)),
            pl.BlockSpec((None, None, tq, 1), lambda b, h, qi: (b, h, qi, 0)),
            pl.BlockSpec((None, None, nq, 1, tq), lambda b, h, qi: (b, h, 0, 0, 0)),
        ],
        out_specs=pl.BlockSpec((tq, FOX_HEAD_DIM), lambda b, h, qi: (b * nq + qi, h)),
        scratch_shapes=[pltpu.VMEM((tq, 1), F32), pltpu.VMEM((tq, 1), F32),
                        pltpu.VMEM((tq, FOX_HEAD_DIM), F32)],
        compiler_params=pltpu.CompilerParams(
            dimension_semantics=("arbitrary", "arbitrary", "arbitrary"),
            vmem_limit_bytes=_vmem_limit(blocks, temps)),
        name="fox_attn",
    )(qkv, qkv, qkv, f_q, f_k)


def _fox_out_kernel(o_ref, w_ref, x_ref, y_ref):
    y_ref[...] = x_ref[...] + jnp.dot(o_ref[...], w_ref[...], preferred_element_type=F32)


def _fox_out(o, w_out, x, *, tm=512):
    n, d = x.shape
    assert n % tm == 0
    blocks = tm * d * 2 + d * d * 2 + 2 * tm * d * 4
    return pl.pallas_call(
        _fox_out_kernel,
        out_shape=jax.ShapeDtypeStruct((n, d), F32),
        grid=(n // tm,),
        in_specs=[
            pl.BlockSpec((tm, d), lambda i: (i, 0)),
            pl.BlockSpec((d, d), lambda i: (0, 0)),
            pl.BlockSpec((tm, d), lambda i: (i, 0)),
        ],
        out_specs=pl.BlockSpec((tm, d), lambda i: (i, 0)),
        compiler_params=pltpu.CompilerParams(
            dimension_semantics=("arbitrary",),
            vmem_limit_bytes=_vmem_limit(blocks, tm * d * 4)),
        name="fox_out",
    )(o, w_out, x)


def _fox(x, gain, w_in, b_f, q_gain, k_gain, w_out, *, batch, seq, tq=512):
    n, d = x.shape
    heads = d // FOX_HEAD_DIM
    assert heads <= V7X_LANES
    w_qkv = w_in[:, :3 * d].astype(BF16)
    w_f = jnp.pad(w_in[:, 3 * d:], ((0, 0), (0, V7X_LANES - heads))).astype(BF16)
    b_pad = jnp.pad(b_f, (0, V7X_LANES - heads)).reshape(1, V7X_LANES)
    qkv, f_cum = _fox_proj(x, gain, w_qkv, w_f, b_pad, q_gain, k_gain, seq=seq)
    f_heads = f_cum[:, :heads].reshape(batch, seq, heads).transpose(0, 2, 1)
    f_q = f_heads.reshape(batch, heads, seq, 1)
    f_k = f_heads.reshape(batch, heads, seq // tq, 1, tq)
    o = _fox_attn(qkv, f_q, f_k, batch=batch, seq=seq, heads=heads, tq=tq)
    return _fox_out(o, w_out.astype(BF16), x)


def kernel(x, ffn1_norm, ffn1_w_gate, ffn1_w_up, ffn1_w_down, mix_norm, pool_w, pool_scale, fox_w_in, fox_b_f,
           fox_q_gain, fox_k_gain, fox_w_out, ffn2_norm, ffn2_w_gate, ffn2_w_up, ffn2_w_down):
    batch, seq, d = x.shape
    depth = ffn1_norm.shape[0]
    n_mixers = 2
    h = x.reshape(batch * seq, d)
    for i in range(depth):
        h = _ffn(h, ffn1_norm[i], ffn1_w_gate[i].astype(BF16), ffn1_w_up[i].astype(BF16),
                 ffn1_w_down[i].astype(BF16))
        j = i // n_mixers
        if i % n_mixers == 0:
            h = _pool(h, mix_norm[i], pool_w[j].astype(BF16), pool_scale[j], seq=seq)
        else:
            h = _fox(h, mix_norm[i], fox_w_in[j], fox_b_f[j], fox_q_gain[j], fox_k_gain[j], fox_w_out[j],
                     batch=batch, seq=seq)
        h = _ffn(h, ffn2_norm[i], ffn2_w_gate[i].astype(BF16), ffn2_w_up[i].astype(BF16),
                 ffn2_w_down[i].astype(BF16))
    return h.reshape(batch, seq, d)
```

```python
import functools
import math

import jax
import jax.numpy as jnp
from jax import lax
from jax.experimental import pallas as pl
from jax.experimental.pallas import tpu as pltpu

RMS_EPS = 1e-6
FFN_RESIDUAL_WEIGHT = 0.5
POOL_WINDOWS = (2, 4, 8, 16)
FOX_HEAD_DIM = 128
NEG_LARGE = -1e30
LOG2E = math.log2(math.e)

V7X_LANES = 128
V7X_VMEM_BYTES = 64 * 1024 * 1024
POOL_HALO = 32
POOL_FIRST_ROW = 8
F_TERMS = 3

BF16 = jnp.bfloat16
F32 = jnp.float32


def _vmem_limit(block_bytes, temp_bytes):
    return min(2 * block_bytes + temp_bytes + (4 << 20), V7X_VMEM_BYTES - (2 << 20))


def _rms_norm(x, gain):
    ms = jnp.mean(x * x, axis=-1, keepdims=True)
    return x * lax.rsqrt(ms + RMS_EPS) * gain


def _dot(a, b):
    return jnp.dot(a, b, preferred_element_type=F32)


def _split_bf16(x):
    terms = []
    for _ in range(F_TERMS):
        t = x.astype(BF16)
        terms.append(t)
        x = x - t.astype(F32)
    return terms


def _ffn_kernel(x_ref, g_ref, wg_ref, wu_ref, wd_ref, o_ref, h_ref):
    j = pl.program_id(1)

    @pl.when(j == 0)
    def _():
        h_ref[...] = _rms_norm(x_ref[...], g_ref[...]).astype(BF16)
        o_ref[...] = jnp.zeros_like(o_ref)

    h = h_ref[...]
    gate = _dot(h, wg_ref[...].astype(BF16))
    up = _dot(h, wu_ref[...].astype(BF16))
    act = (gate * jax.nn.sigmoid(gate) * up).astype(BF16)
    o_ref[...] += _dot(act, wd_ref[...].astype(BF16))

    @pl.when(j == pl.num_programs(1) - 1)
    def _():
        o_ref[...] = x_ref[...] + FFN_RESIDUAL_WEIGHT * o_ref[...]


def _ffn(x, gains, w_gate, w_up, w_down, layer, *, tm, tf=256):
    n, d = x.shape
    f = w_gate.shape[2]
    assert n % tm == 0 and f % tf == 0
    blocks = 2 * tm * d * 4 + 3 * d * tf * 4 + d * 4
    temps = tm * d * 2 + 3 * d * tf * 2 + 3 * tm * tf * 4
    return pl.pallas_call(
        _ffn_kernel,
        out_shape=jax.ShapeDtypeStruct((n, d), F32),
        grid=(n // tm, f // tf),
        in_specs=[
            pl.BlockSpec((tm, d), lambda i, j: (i, 0)),
            pl.BlockSpec((None, 1, d), lambda i, j: (layer, 0, 0)),
            pl.BlockSpec((None, d, tf), lambda i, j: (layer, 0, j)),
            pl.BlockSpec((None, d, tf), lambda i, j: (layer, 0, j)),
            pl.BlockSpec((None, tf, d), lambda i, j: (layer, j, 0)),
        ],
        out_specs=pl.BlockSpec((tm, d), lambda i, j: (i, 0)),
        scratch_shapes=[pltpu.VMEM((tm, d), BF16)],
        compiler_params=pltpu.CompilerParams(
            dimension_semantics=("arbitrary", "arbitrary"),
            vmem_limit_bytes=_vmem_limit(blocks, temps)),
        name="ffn",
    )(x, gains.reshape(gains.shape[0], 1, d), w_gate, w_up, w_down)


def _pool_kernel(x_ref, xh_ref, g_ref, w_ref, s_ref, o_ref, h_ref, p_ref, q_ref, *, tiles_per_batch):
    tm, d = x_ref.shape
    group = d // len(POOL_WINDOWS)
    rows = tm + POOL_HALO
    tile_in_batch = lax.rem(pl.program_id(0), tiles_per_batch)

    gain = g_ref[...]
    halo = _rms_norm(xh_ref[...], gain)
    h_ref[0:POOL_HALO, :] = jnp.where(tile_in_batch == 0, 0.0, halo)
    h_ref[POOL_HALO:, :] = _rms_norm(x_ref[...], gain)

    src = h_ref
    first_exact_row = POOL_FIRST_ROW
    for step in range(len(POOL_WINDOWS)):
        shift = 1 << step
        assert POOL_WINDOWS[step] == 2 * shift and shift <= POOL_FIRST_ROW
        first_exact_row += shift if step else 0
        dst = p_ref if step % 2 == 0 else q_ref
        c0 = step * group
        dst[POOL_FIRST_ROW:, c0:] = (src[POOL_FIRST_ROW:, c0:]
                                     + src[POOL_FIRST_ROW - shift:rows - shift, c0:])
        src = dst
    assert first_exact_row <= POOL_HALO

    pos = tile_in_batch * tm + lax.broadcasted_iota(jnp.int32, (tm, 1), 0)
    for gi, window in enumerate(POOL_WINDOWS):
        cols = slice(gi * group, (gi + 1) * group)
        sums = (p_ref if gi % 2 == 0 else q_ref)[POOL_HALO:, cols]
        count = jnp.minimum(pos + 1, window).astype(F32)
        diff = (sums / count - h_ref[POOL_HALO:, cols]).astype(BF16)
        y = _dot(diff, w_ref[gi].astype(BF16))
        o_ref[:, cols] = x_ref[:, cols] + y * s_ref[:, cols]


def _pool(x, gains, layer, w_groups, scales, mixer, *, seq, tm=512):
    n, d = x.shape
    _, groups, gsz, _ = w_groups.shape
    assert seq % tm == 0 and tm % POOL_HALO == 0 and groups * gsz == d
    halo_blocks = tm // POOL_HALO
    blocks = 2 * tm * d * 4 + POOL_HALO * d * 4 + groups * gsz * gsz * 4 + 2 * d * 4
    temps = 3 * (tm + POOL_HALO) * d * 4 + 2 * tm * d * 4
    return pl.pallas_call(
        functools.partial(_pool_kernel, tiles_per_batch=seq // tm),
        out_shape=jax.ShapeDtypeStruct((n, d), F32),
        grid=(n // tm,),
        in_specs=[
            pl.BlockSpec((tm, d), lambda i: (i, 0)),
            pl.BlockSpec((POOL_HALO, d), lambda i: (jnp.maximum(i * halo_blocks - 1, 0), 0)),
            pl.BlockSpec((None, 1, d), lambda i: (layer, 0, 0)),
            pl.BlockSpec((None, groups, gsz, gsz), lambda i: (mixer, 0, 0, 0)),
            pl.BlockSpec((None, 1, d), lambda i: (mixer, 0, 0)),
        ],
        out_specs=pl.BlockSpec((tm, d), lambda i: (i, 0)),
        scratch_shapes=[pltpu.VMEM((tm + POOL_HALO, d), F32)] * 3,
        compiler_params=pltpu.CompilerParams(
            dimension_semantics=("arbitrary",),
            vmem_limit_bytes=_vmem_limit(blocks, temps)),
        name="pool",
    )(x, x, gains.reshape(gains.shape[0], 1, d), w_groups, scales.reshape(scales.shape[0], 1, d))


def _log_sigmoid(z):
    return jnp.minimum(z, 0.0) - jnp.log1p(jnp.exp(-jnp.abs(z)))


def _fox_proj_kernel(x_ref, g_ref, w_ref, wvt_ref, wf_ref, bf_ref, gq_ref, gk_ref, qk_ref, vt_ref, f_ref,
                     h_ref, carry_ref, *, tiles_per_batch, qk_tiles, chunk):
    i = pl.program_id(0)
    j = pl.program_id(1)
    tm, tn = qk_ref.shape
    tk = vt_ref.shape[2]

    @pl.when(j == 0)
    def _():
        h_ref[...] = _rms_norm(x_ref[...], g_ref[...]).astype(BF16)

        @pl.when(lax.rem(i, tiles_per_batch) == 0)
        def _():
            carry_ref[...] = jnp.zeros_like(carry_ref)

        r = lax.broadcasted_iota(jnp.int32, (chunk, chunk), 0)
        c = lax.broadcasted_iota(jnp.int32, (chunk, chunk), 1)
        tri = (c <= r).astype(BF16)
        carry = carry_ref[...]
        for c0 in range(0, tm, chunk):
            logit = _dot(h_ref[c0:c0 + chunk, :], wf_ref[...]) + bf_ref[...]
            prefix = sum(_dot(tri, t) for t in _split_bf16(_log_sigmoid(logit)))
            total = prefix + carry
            carry = total[chunk - 1:chunk, :]
            for t, term in enumerate(_split_bf16(total * LOG2E)):
                f_ref[c0:c0 + chunk, t * V7X_LANES:(t + 1) * V7X_LANES] = term
        carry_ref[...] = carry

    @pl.when(j < qk_tiles)
    def _():
        acc = _dot(h_ref[...], w_ref[...].astype(BF16))
        is_q = j < qk_tiles // 2
        gain = jnp.where(is_q, gq_ref[...], gk_ref[...])
        post = jnp.where(is_q, LOG2E / math.sqrt(FOX_HEAD_DIM), 1.0)
        for c0 in range(0, tn, FOX_HEAD_DIM):
            y = _rms_norm(acc[:, c0:c0 + FOX_HEAD_DIM], gain) * post
            qk_ref[:, c0:c0 + FOX_HEAD_DIM] = y.astype(BF16)

    @pl.when(j >= qk_tiles)
    def _():
        vt = lax.dot_general(wvt_ref[...].astype(BF16), h_ref[...], (((1,), (1,)), ((), ())),
                             preferred_element_type=F32)
        for b in range(tm // tk):
            vt_ref[b] = vt[:, b * tk:(b + 1) * tk].astype(BF16)


def _fox_proj(x, gains, layer, w_in, w_vt, w_f, b_f, q_gains, k_gains, mixer, *, seq, tm, tk, tn=512, chunk=256):
    n, d = x.shape
    assert seq % tm == 0 and tm % tk == 0 and tm % chunk == 0 and d % tn == 0 and tn % FOX_HEAD_DIM == 0
    qk_tiles = 2 * d // tn
    f_cols = F_TERMS * V7X_LANES
    blocks = (tm * d * 4 + 2 * d * tn * 4 + d * V7X_LANES * 2 + tm * tn * 2 + tm * tn * 2 + tm * f_cols * 2)
    temps = tm * d * 2 + d * tn * 2 + 3 * tm * tn * 4 + 4 * chunk * chunk * 4
    return pl.pallas_call(
        functools.partial(_fox_proj_kernel, tiles_per_batch=seq // tm, qk_tiles=qk_tiles, chunk=chunk),
        out_shape=(jax.ShapeDtypeStruct((n, 2 * d), BF16),
                   jax.ShapeDtypeStruct((n // tk, d, tk), BF16),
                   jax.ShapeDtypeStruct((n, f_cols), BF16)),
        grid=(n // tm, qk_tiles + d // tn),
        in_specs=[
            pl.BlockSpec((tm, d), lambda i, j: (i, 0)),
            pl.BlockSpec((None, 1, d), lambda i, j: (layer, 0, 0)),
            pl.BlockSpec((None, d, tn), lambda i, j: (mixer, 0, jnp.minimum(j, qk_tiles - 1))),
            pl.BlockSpec((None, tn, d), lambda i, j: (mixer, jnp.maximum(j - qk_tiles, 0), 0)),
            pl.BlockSpec((None, d, V7X_LANES), lambda i, j: (mixer, 0, 0)),
            pl.BlockSpec((None, 1, V7X_LANES), lambda i, j: (mixer, 0, 0)),
            pl.BlockSpec((None, 1, FOX_HEAD_DIM), lambda i, j: (mixer, 0, 0)),
            pl.BlockSpec((None, 1, FOX_HEAD_DIM), lambda i, j: (mixer, 0, 0)),
        ],
        out_specs=(pl.BlockSpec((tm, tn), lambda i, j: (i, jnp.minimum(j, qk_tiles - 1))),
                   pl.BlockSpec((tm // tk, tn, tk), lambda i, j: (i, jnp.maximum(j - qk_tiles, 0), 0)),
                   pl.BlockSpec((tm, f_cols), lambda i, j: (i, 0))),
        scratch_shapes=[pltpu.VMEM((tm, d), BF16), pltpu.VMEM((1, V7X_LANES), F32)],
        compiler_params=pltpu.CompilerParams(
            dimension_semantics=("arbitrary", "arbitrary"),
            vmem_limit_bytes=_vmem_limit(blocks, temps)),
        name="fox_proj",
    )(x, gains.reshape(gains.shape[0], 1, d), w_in, w_vt, w_f, b_f,
      q_gains.reshape(q_gains.shape[0], 1, -1), k_gains.reshape(k_gains.shape[0], 1, -1))


def _fox_attn_kernel(q_ref, k_ref, vt_ref, f_ref, o_ref, kaug_ref, qaug_ref, s_ref, cmax_ref, m_ref, l_ref, acc_ref,
                     *, tk):
    head = pl.program_id(1)
    qi = pl.program_id(2)
    tq = q_ref.shape[0]
    seq = k_ref.shape[0]
    assert tq == tk

    r = lax.broadcasted_iota(jnp.int32, (F_TERMS * V7X_LANES, V7X_LANES), 0)
    c = lax.broadcasted_iota(jnp.int32, (F_TERMS * V7X_LANES, V7X_LANES), 1)
    sel_q = (r == c * V7X_LANES + head).astype(BF16)
    sel_k = -(r == (c - F_TERMS) * V7X_LANES + head).astype(BF16)
    lane = lax.broadcasted_iota(jnp.int32, (1, V7X_LANES), 1)
    ones_q = ((lane >= F_TERMS) & (lane < 2 * F_TERMS)).astype(F32)
    ones_k = (lane < F_TERMS).astype(F32)

    @pl.when(qi == 0)
    def _():
        def fill(kb, carry):
            rows = pl.ds(pl.multiple_of(kb * tk, tk), tk)
            kaug_ref[rows, 0:FOX_HEAD_DIM] = k_ref[rows, :]
            kaug_ref[rows, FOX_HEAD_DIM:] = (_dot(f_ref[rows, :], sel_k) + ones_k).astype(BF16)
            return carry
        lax.fori_loop(0, seq // tk, fill, 0)

    q_rows = pl.ds(pl.multiple_of(qi * tq, tq), tq)
    qaug_ref[:, 0:FOX_HEAD_DIM] = q_ref[...]
    qaug_ref[:, FOX_HEAD_DIM:] = (_dot(f_ref[q_rows, :], sel_q) + ones_q).astype(BF16)
    m_ref[...] = jnp.full_like(m_ref, NEG_LARGE)
    l_ref[...] = jnp.zeros_like(l_ref)
    acc_ref[...] = jnp.zeros_like(acc_ref)

    def scores(kb, masked):
        rows = pl.ds(pl.multiple_of(kb * tk, tk), tk)
        s = lax.dot_general(kaug_ref[rows, :], qaug_ref[...], (((1,), (1,)), ((), ())),
                            preferred_element_type=F32)
        if masked:
            key = lax.broadcasted_iota(jnp.int32, (tk, tq), 0)
            qry = lax.broadcasted_iota(jnp.int32, (tk, tq), 1)
            s = jnp.where(key <= qry, s, NEG_LARGE)
        cmax_ref[...] = jnp.max(s, axis=0, keepdims=True)
        s_ref[...] = s

    def accumulate(kb):
        m_old = m_ref[...]
        m_new = jnp.maximum(m_old, cmax_ref[...])
        alpha = jnp.exp2(m_old - m_new)
        p = jnp.exp2(s_ref[...] - m_new)
        l_ref[...] = alpha * l_ref[...] + jnp.sum(p, axis=0, keepdims=True)
        acc_ref[...] = alpha * acc_ref[...] + _dot(vt_ref[kb], p.astype(BF16))
        m_ref[...] = m_new

    @pl.when(qi == 0)
    def _():
        scores(0, True)

    @pl.when(qi > 0)
    def _():
        scores(0, False)

        def body(kb, carry):
            accumulate(kb)
            scores(kb + 1, False)
            return carry
        lax.fori_loop(0, qi - 1, body, 0)
        accumulate(qi - 1)
        scores(qi, True)

    accumulate(qi)
    o_ref[...] = (acc_ref[...] / l_ref[...]).T.astype(BF16)


def _fox_attn(qk, vt, f_terms, *, batch, seq, heads, tq):
    n = qk.shape[0]
    d = heads * FOX_HEAD_DIM
    nq = seq // tq
    f_cols = f_terms.shape[1]
    assert seq % tq == 0 and vt.shape == (n // tq, d, tq)
    blocks = 2 * tq * FOX_HEAD_DIM * 2 + 2 * seq * FOX_HEAD_DIM * 2 + seq * f_cols * 2
    temps = seq * 2 * FOX_HEAD_DIM * 2 + tq * FOX_HEAD_DIM * 4 + 6 * tq * tq * 4
    return pl.pallas_call(
        functools.partial(_fox_attn_kernel, tk=tq),
        out_shape=jax.ShapeDtypeStruct((n, d), BF16),
        grid=(batch, heads, nq),
        in_specs=[
            pl.BlockSpec((tq, FOX_HEAD_DIM), lambda b, h, qi: (b * nq + qi, h)),
            pl.BlockSpec((seq, FOX_HEAD_DIM), lambda b, h, qi: (b, heads + h)),
            pl.BlockSpec((nq, FOX_HEAD_DIM, tq), lambda b, h, qi: (b, h, 0)),
            pl.BlockSpec((seq, f_cols), lambda b, h, qi: (b, 0)),
        ],
        out_specs=pl.BlockSpec((tq, FOX_HEAD_DIM), lambda b, h, qi: (b * nq + qi, h)),
        scratch_shapes=[pltpu.VMEM((seq, 2 * FOX_HEAD_DIM), BF16), pltpu.VMEM((tq, 2 * FOX_HEAD_DIM), BF16),
                        pltpu.VMEM((tq, tq), F32), pltpu.VMEM((1, tq), F32), pltpu.VMEM((1, tq), F32),
                        pltpu.VMEM((1, tq), F32), pltpu.VMEM((FOX_HEAD_DIM, tq), F32)],
        compiler_params=pltpu.CompilerParams(
            dimension_semantics=("arbitrary", "arbitrary", "arbitrary"),
            vmem_limit_bytes=_vmem_limit(blocks, temps)),
        name="fox_attn",
    )(qk, qk, vt, f_terms)


def _fox_out_kernel(o_ref, w_ref, x_ref, y_ref):
    y_ref[...] = x_ref[...] + _dot(o_ref[...], w_ref[...].astype(BF16))


def _fox_out(o, w_out, mixer, x, *, tm, tn=512):
    n, d = x.shape
    assert n % tm == 0 and d % tn == 0
    blocks = tm * d * 2 + d * tn * 4 + 2 * tm * tn * 4
    temps = d * tn * 2 + tm * tn * 4
    return pl.pallas_call(
        _fox_out_kernel,
        out_shape=jax.ShapeDtypeStruct((n, d), F32),
        grid=(n // tm, d // tn),
        in_specs=[
            pl.BlockSpec((tm, d), lambda i, j: (i, 0)),
            pl.BlockSpec((None, d, tn), lambda i, j: (mixer, 0, j)),
            pl.BlockSpec((tm, tn), lambda i, j: (i, j)),
        ],
        out_specs=pl.BlockSpec((tm, tn), lambda i, j: (i, j)),
        compiler_params=pltpu.CompilerParams(
            dimension_semantics=("arbitrary", "arbitrary"),
            vmem_limit_bytes=_vmem_limit(blocks, temps)),
        name="fox_out",
    )(o, w_out, x)


def _fox(x, gains, layer, w_in, w_vt, w_f, b_f, q_gains, k_gains, w_out, mixer, *, batch, seq, tm, tq):
    heads = x.shape[1] // FOX_HEAD_DIM
    qk, vt, f_terms = _fox_proj(x, gains, layer, w_in, w_vt, w_f, b_f, q_gains, k_gains, mixer,
                                seq=seq, tm=tm, tk=tq)
    o = _fox_attn(qk, vt, f_terms, batch=batch, seq=seq, heads=heads, tq=tq)
    return _fox_out(o, w_out, mixer, x, tm=tm)


def kernel(x, ffn1_norm, ffn1_w_gate, ffn1_w_up, ffn1_w_down, mix_norm, pool_w, pool_scale, fox_w_in, fox_b_f,
           fox_q_gain, fox_k_gain, fox_w_out, ffn2_norm, ffn2_w_gate, ffn2_w_up, ffn2_w_down):
    batch, seq, d = x.shape
    depth = ffn1_norm.shape[0]
    n_mixers = 2
    heads = d // FOX_HEAD_DIM
    tm = min(1024, seq)
    tq = min(512, seq)
    assert heads <= V7X_LANES

    w_vt = jnp.swapaxes(fox_w_in[:, :, 2 * d:3 * d], 1, 2)
    w_f = jnp.pad(fox_w_in[:, :, 3 * d:], ((0, 0), (0, 0), (0, V7X_LANES - heads))).astype(BF16)
    b_f = jnp.pad(fox_b_f, ((0, 0), (0, V7X_LANES - heads))).reshape(-1, 1, V7X_LANES)

    h = x.reshape(batch * seq, d)
    for i in range(depth):
        h = _ffn(h, ffn1_norm, ffn1_w_gate, ffn1_w_up, ffn1_w_down, i, tm=tm)
        j = i // n_mixers
        if i % n_mixers == 0:
            h = _pool(h, mix_norm, i, pool_w, pool_scale, j, seq=seq, tm=min(512, seq))
        else:
            h = _fox(h, mix_norm, i, fox_w_in, w_vt, w_f, b_f, fox_q_gain, fox_k_gain, fox_w_out, j,
                     batch=batch, seq=seq, tm=tm, tq=tq)
        h = _ffn(h, ffn2_norm, ffn2_w_gate, ffn2_w_up, ffn2_w_down, i, tm=tm)
    return h.reshape(batch, seq, d)
```

```python
import functools
import math

import jax
import jax.numpy as jnp
from jax import lax
from jax.experimental import pallas as pl
from jax.experimental.pallas import tpu as pltpu

RMS_EPS = 1e-6
FFN_RESIDUAL_WEIGHT = 0.5
POOL_WINDOWS = (2, 4, 8, 16)
FOX_HEAD_DIM = 128
NEG_LARGE = -1e30
LOG2E = math.log2(math.e)

SKIP_LOG2_MARGIN = 160.0
QK_BOUND_SLACK = 1.01

V7X_LANES = 128
V7X_SUBLANES = 8
V7X_VMEM_BYTES = 64 * 1024 * 1024
POOL_HALO = 32
POOL_FIRST_ROW = 8
F_TERMS = 3

BF16 = jnp.bfloat16
F32 = jnp.float32


def _vmem_limit(block_bytes, temp_bytes):
    return min(2 * block_bytes + temp_bytes + (4 << 20), V7X_VMEM_BYTES - (2 << 20))


def _rms_norm(x, gain):
    ms = jnp.mean(x * x, axis=-1, keepdims=True)
    return x * lax.rsqrt(ms + RMS_EPS) * gain


def _dot(a, b):
    return jnp.dot(a, b, preferred_element_type=F32)


def _split_bf16(x):
    terms = []
    for _ in range(F_TERMS):
        t = x.astype(BF16)
        terms.append(t)
        x = x - t.astype(F32)
    return terms


def _ffn_kernel(x_ref, g_ref, wg_ref, wu_ref, wd_ref, o_ref, h_ref):
    j = pl.program_id(1)

    @pl.when(j == 0)
    def _():
        h_ref[...] = _rms_norm(x_ref[...], g_ref[...]).astype(BF16)
        o_ref[...] = jnp.zeros_like(o_ref)

    h = h_ref[...]
    gate = _dot(h, wg_ref[...].astype(BF16))
    up = _dot(h, wu_ref[...].astype(BF16))
    act = (gate * jax.nn.sigmoid(gate) * up).astype(BF16)
    o_ref[...] += _dot(act, wd_ref[...].astype(BF16))

    @pl.when(j == pl.num_programs(1) - 1)
    def _():
        o_ref[...] = x_ref[...] + FFN_RESIDUAL_WEIGHT * o_ref[...]


def _ffn(x, gains, w_gate, w_up, w_down, layer, *, tm, tf=256):
    n, d = x.shape
    f = w_gate.shape[2]
    assert n % tm == 0 and f % tf == 0
    blocks = 2 * tm * d * 4 + 3 * d * tf * 4 + d * 4
    temps = tm * d * 2 + 3 * d * tf * 2 + 3 * tm * tf * 4
    return pl.pallas_call(
        _ffn_kernel,
        out_shape=jax.ShapeDtypeStruct((n, d), F32),
        grid=(n // tm, f // tf),
        in_specs=[
            pl.BlockSpec((tm, d), lambda i, j: (i, 0)),
            pl.BlockSpec((None, 1, d), lambda i, j: (layer, 0, 0)),
            pl.BlockSpec((None, d, tf), lambda i, j: (layer, 0, j)),
            pl.BlockSpec((None, d, tf), lambda i, j: (layer, 0, j)),
            pl.BlockSpec((None, tf, d), lambda i, j: (layer, j, 0)),
        ],
        out_specs=pl.BlockSpec((tm, d), lambda i, j: (i, 0)),
        scratch_shapes=[pltpu.VMEM((tm, d), BF16)],
        compiler_params=pltpu.CompilerParams(
            dimension_semantics=("arbitrary", "arbitrary"),
            vmem_limit_bytes=_vmem_limit(blocks, temps)),
        name="ffn",
    )(x, gains.reshape(gains.shape[0], 1, d), w_gate, w_up, w_down)


def _pool_kernel(x_ref, xh_ref, g_ref, w_ref, s_ref, o_ref, h_ref, p_ref, q_ref, *, tiles_per_batch):
    tm, d = x_ref.shape
    group = d // len(POOL_WINDOWS)
    rows = tm + POOL_HALO
    tile_in_batch = lax.rem(pl.program_id(0), tiles_per_batch)

    gain = g_ref[...]
    halo = _rms_norm(xh_ref[...], gain)
    h_ref[0:POOL_HALO, :] = jnp.where(tile_in_batch == 0, 0.0, halo)
    h_ref[POOL_HALO:, :] = _rms_norm(x_ref[...], gain)

    src = h_ref
    first_exact_row = POOL_FIRST_ROW
    for step in range(len(POOL_WINDOWS)):
        shift = 1 << step
        assert POOL_WINDOWS[step] == 2 * shift and shift <= POOL_FIRST_ROW
        first_exact_row += shift if step else 0
        dst = p_ref if step % 2 == 0 else q_ref
        c0 = step * group
        dst[POOL_FIRST_ROW:, c0:] = (src[POOL_FIRST_ROW:, c0:]
                                     + src[POOL_FIRST_ROW - shift:rows - shift, c0:])
        src = dst
    assert first_exact_row <= POOL_HALO

    pos = tile_in_batch * tm + lax.broadcasted_iota(jnp.int32, (tm, 1), 0)
    for gi, window in enumerate(POOL_WINDOWS):
        cols = slice(gi * group, (gi + 1) * group)
        sums = (p_ref if gi % 2 == 0 else q_ref)[POOL_HALO:, cols]
        count = jnp.minimum(pos + 1, window).astype(F32)
        diff = (sums / count - h_ref[POOL_HALO:, cols]).astype(BF16)
        y = _dot(diff, w_ref[gi].astype(BF16))
        o_ref[:, cols] = x_ref[:, cols] + y * s_ref[:, cols]


def _pool(x, gains, layer, w_groups, scales, mixer, *, seq, tm=512):
    n, d = x.shape
    _, groups, gsz, _ = w_groups.shape
    assert seq % tm == 0 and tm % POOL_HALO == 0 and groups * gsz == d
    halo_blocks = tm // POOL_HALO
    blocks = 2 * tm * d * 4 + POOL_HALO * d * 4 + groups * gsz * gsz * 4 + 2 * d * 4
    temps = 3 * (tm + POOL_HALO) * d * 4 + 2 * tm * d * 4
    return pl.pallas_call(
        functools.partial(_pool_kernel, tiles_per_batch=seq // tm),
        out_shape=jax.ShapeDtypeStruct((n, d), F32),
        grid=(n // tm,),
        in_specs=[
            pl.BlockSpec((tm, d), lambda i: (i, 0)),
            pl.BlockSpec((POOL_HALO, d), lambda i: (jnp.maximum(i * halo_blocks - 1, 0), 0)),
            pl.BlockSpec((None, 1, d), lambda i: (layer, 0, 0)),
            pl.BlockSpec((None, groups, gsz, gsz), lambda i: (mixer, 0, 0, 0)),
            pl.BlockSpec((None, 1, d), lambda i: (mixer, 0, 0)),
        ],
        out_specs=pl.BlockSpec((tm, d), lambda i: (i, 0)),
        scratch_shapes=[pltpu.VMEM((tm + POOL_HALO, d), F32)] * 3,
        compiler_params=pltpu.CompilerParams(
            dimension_semantics=("arbitrary",),
            vmem_limit_bytes=_vmem_limit(blocks, temps)),
        name="pool",
    )(x, x, gains.reshape(gains.shape[0], 1, d), w_groups, scales.reshape(scales.shape[0], 1, d))


def _log_sigmoid(z):
    return jnp.minimum(z, 0.0) - jnp.log1p(jnp.exp(-jnp.abs(z)))


def _fox_proj_kernel(x_ref, g_ref, w_ref, wvt_ref, wf_ref, bf_ref, gq_ref, gk_ref, qk_ref, vt_ref, f_ref, ks_ref,
                     h_ref, carry_ref, fmin_ref, *, tiles_per_batch, qk_tiles, chunk):
    i = pl.program_id(0)
    j = pl.program_id(1)
    tm, tn = qk_ref.shape
    tk = vt_ref.shape[2]
    blocks_per_tile = tm // tk

    @pl.when(j == 0)
    def _():
        h_ref[...] = _rms_norm(x_ref[...], g_ref[...]).astype(BF16)
        tile_in_batch = lax.rem(i, tiles_per_batch)

        @pl.when(tile_in_batch == 0)
        def _():
            carry_ref[...] = jnp.zeros_like(carry_ref)
            fmin_ref[...] = jnp.zeros_like(fmin_ref)

        r = lax.broadcasted_iota(jnp.int32, (chunk, chunk), 0)
        c = lax.broadcasted_iota(jnp.int32, (chunk, chunk), 1)
        tri = (c <= r).astype(BF16)
        carry = carry_ref[...]
        block_max = [None] * blocks_per_tile
        block_min = [None] * blocks_per_tile
        for c0 in range(0, tm, chunk):
            logit = _dot(h_ref[c0:c0 + chunk, :], wf_ref[...]) + bf_ref[...]
            prefix = sum(_dot(tri, t) for t in _split_bf16(_log_sigmoid(logit)))
            total = prefix + carry
            carry = total[chunk - 1:chunk, :]
            f2 = total * LOG2E
            for t, term in enumerate(_split_bf16(f2)):
                f_ref[c0:c0 + chunk, t * V7X_LANES:(t + 1) * V7X_LANES] = term
            blk = c0 // tk
            hi = jnp.max(f2, axis=0, keepdims=True)
            lo = jnp.min(f2, axis=0, keepdims=True)
            block_max[blk] = hi if block_max[blk] is None else jnp.maximum(block_max[blk], hi)
            block_min[blk] = lo if block_min[blk] is None else jnp.minimum(block_min[blk], lo)
        carry_ref[...] = carry

        qk_bound = (LOG2E * math.sqrt(FOX_HEAD_DIM) * QK_BOUND_SLACK
                    * jnp.max(jnp.abs(gq_ref[...]), axis=-1, keepdims=True)
                    * jnp.max(jnp.abs(gk_ref[...]), axis=-1, keepdims=True))
        n_blocks = fmin_ref.shape[0]
        row = lax.broadcasted_iota(jnp.int32, fmin_ref.shape, 0)
        ks_ref[...] = jnp.zeros_like(ks_ref)
        for blk in range(blocks_per_tile):
            qb = tile_in_batch * blocks_per_tile + blk
            fmin_ref[pl.ds(qb, 1), :] = block_min[blk]
            gap = 2.0 * qk_bound + block_max[blk] - fmin_ref[...]
            needed = (gap > -SKIP_LOG2_MARGIN) | (row >= qb)
            ks_ref[blk:blk + 1, :] = jnp.min(jnp.where(needed, row, n_blocks), axis=0, keepdims=True)

    @pl.when(j < qk_tiles)
    def _():
        acc = _dot(h_ref[...], w_ref[...].astype(BF16))
        is_q = j < qk_tiles // 2
        gain = jnp.where(is_q, gq_ref[...], gk_ref[...])
        post = jnp.where(is_q, LOG2E / math.sqrt(FOX_HEAD_DIM), 1.0)
        for c0 in range(0, tn, FOX_HEAD_DIM):
            y = _rms_norm(acc[:, c0:c0 + FOX_HEAD_DIM], gain) * post
            qk_ref[:, c0:c0 + FOX_HEAD_DIM] = y.astype(BF16)

    @pl.when(j >= qk_tiles)
    def _():
        vt = lax.dot_general(wvt_ref[...].astype(BF16), h_ref[...], (((1,), (1,)), ((), ())),
                             preferred_element_type=F32)
        for b in range(tm // tk):
            vt_ref[b] = vt[:, b * tk:(b + 1) * tk].astype(BF16)


def _fox_proj(x, gains, layer, w_in, w_vt, w_f, b_f, q_gains, k_gains, mixer, *, seq, tm, tk, tn=512, chunk=256):
    n, d = x.shape
    assert seq % tm == 0 and tm % tk == 0 and tk % chunk == 0 and d % tn == 0 and tn % FOX_HEAD_DIM == 0
    assert tm // tk <= V7X_SUBLANES
    qk_tiles = 2 * d // tn
    f_cols = F_TERMS * V7X_LANES
    blocks = (tm * d * 4 + 2 * d * tn * 4 + d * V7X_LANES * 2 + tm * tn * 2 + tm * tn * 2 + tm * f_cols * 2)
    temps = tm * d * 2 + d * tn * 2 + 3 * tm * tn * 4 + 4 * chunk * chunk * 4
    return pl.pallas_call(
        functools.partial(_fox_proj_kernel, tiles_per_batch=seq // tm, qk_tiles=qk_tiles, chunk=chunk),
        out_shape=(jax.ShapeDtypeStruct((n, 2 * d), BF16),
                   jax.ShapeDtypeStruct((n // tk, d, tk), BF16),
                   jax.ShapeDtypeStruct((n, f_cols), BF16),
                   jax.ShapeDtypeStruct((n // tm, V7X_SUBLANES, V7X_LANES), jnp.int32)),
        grid=(n // tm, qk_tiles + d // tn),
        in_specs=[
            pl.BlockSpec((tm, d), lambda i, j: (i, 0)),
            pl.BlockSpec((None, 1, d), lambda i, j: (layer, 0, 0)),
            pl.BlockSpec((None, d, tn), lambda i, j: (mixer, 0, jnp.minimum(j, qk_tiles - 1))),
            pl.BlockSpec((None, tn, d), lambda i, j: (mixer, jnp.maximum(j - qk_tiles, 0), 0)),
            pl.BlockSpec((None, d, V7X_LANES), lambda i, j: (mixer, 0, 0)),
            pl.BlockSpec((None, 1, V7X_LANES), lambda i, j: (mixer, 0, 0)),
            pl.BlockSpec((None, 1, FOX_HEAD_DIM), lambda i, j: (mixer, 0, 0)),
            pl.BlockSpec((None, 1, FOX_HEAD_DIM), lambda i, j: (mixer, 0, 0)),
        ],
        out_specs=(pl.BlockSpec((tm, tn), lambda i, j: (i, jnp.minimum(j, qk_tiles - 1))),
                   pl.BlockSpec((tm // tk, tn, tk), lambda i, j: (i, jnp.maximum(j - qk_tiles, 0), 0)),
                   pl.BlockSpec((tm, f_cols), lambda i, j: (i, 0)),
                   pl.BlockSpec((None, V7X_SUBLANES, V7X_LANES), lambda i, j: (i, 0, 0))),
        scratch_shapes=[pltpu.VMEM((tm, d), BF16), pltpu.VMEM((1, V7X_LANES), F32),
                        pltpu.VMEM((seq // tk, V7X_LANES), F32)],
        compiler_params=pltpu.CompilerParams(
            dimension_semantics=("arbitrary", "arbitrary"),
            vmem_limit_bytes=_vmem_limit(blocks, temps)),
        name="fox_proj",
    )(x, gains.reshape(gains.shape[0], 1, d), w_in, w_vt, w_f, b_f,
      q_gains.reshape(q_gains.shape[0], 1, -1), k_gains.reshape(k_gains.shape[0], 1, -1))


def _fox_attn_kernel(first_ref, q_ref, k_ref, vt_ref, f_ref, o_ref, kaug_ref, qaug_ref, s_ref, cmax_ref, m_ref,
                     l_ref, acc_ref, *, tk):
    head = pl.program_id(1)
    qi = pl.program_id(2)
    tq = q_ref.shape[0]
    seq = k_ref.shape[0]
    assert tq == tk
    first = first_ref[(pl.program_id(0) * pl.num_programs(1) + head) * pl.num_programs(2) + qi]

    r = lax.broadcasted_iota(jnp.int32, (F_TERMS * V7X_LANES, V7X_LANES), 0)
    c = lax.broadcasted_iota(jnp.int32, (F_TERMS * V7X_LANES, V7X_LANES), 1)
    sel_q = (r == c * V7X_LANES + head).astype(BF16)
    sel_k = -(r == (c - F_TERMS) * V7X_LANES + head).astype(BF16)
    lane = lax.broadcasted_iota(jnp.int32, (1, V7X_LANES), 1)
    ones_q = ((lane >= F_TERMS) & (lane < 2 * F_TERMS)).astype(F32)
    ones_k = (lane < F_TERMS).astype(F32)

    @pl.when(qi == 0)
    def _():
        def fill(kb, carry):
            rows = pl.ds(pl.multiple_of(kb * tk, tk), tk)
            kaug_ref[rows, 0:FOX_HEAD_DIM] = k_ref[rows, :]
            kaug_ref[rows, FOX_HEAD_DIM:] = (_dot(f_ref[rows, :], sel_k) + ones_k).astype(BF16)
            return carry
        lax.fori_loop(0, seq // tk, fill, 0)

    q_rows = pl.ds(pl.multiple_of(qi * tq, tq), tq)
    qaug_ref[:, 0:FOX_HEAD_DIM] = q_ref[...]
    qaug_ref[:, FOX_HEAD_DIM:] = (_dot(f_ref[q_rows, :], sel_q) + ones_q).astype(BF16)
    m_ref[...] = jnp.full_like(m_ref, NEG_LARGE)
    l_ref[...] = jnp.zeros_like(l_ref)
    acc_ref[...] = jnp.zeros_like(acc_ref)

    def scores(kb, masked):
        rows = pl.ds(pl.multiple_of(kb * tk, tk), tk)
        s = lax.dot_general(kaug_ref[rows, :], qaug_ref[...], (((1,), (1,)), ((), ())),
                            preferred_element_type=F32)
        if masked:
            key = lax.broadcasted_iota(jnp.int32, (tk, tq), 0)
            qry = lax.broadcasted_iota(jnp.int32, (tk, tq), 1)
            s = jnp.where(key <= qry, s, NEG_LARGE)
        cmax_ref[...] = jnp.max(s, axis=0, keepdims=True)
        s_ref[...] = s

    def accumulate(kb):
        m_old = m_ref[...]
        m_new = jnp.maximum(m_old, cmax_ref[...])
        alpha = jnp.exp2(m_old - m_new)
        p = jnp.exp2(s_ref[...] - m_new)
        l_ref[...] = alpha * l_ref[...] + jnp.sum(p, axis=0, keepdims=True)
        acc_ref[...] = alpha * acc_ref[...] + _dot(vt_ref[kb], p.astype(BF16))
        m_ref[...] = m_new

    @pl.when(first >= qi)
    def _():
        scores(qi, True)

    @pl.when(first < qi)
    def _():
        scores(first, False)

        def body(kb, carry):
            accumulate(kb)
            scores(kb + 1, False)
            return carry
        lax.fori_loop(first, qi - 1, body, 0)
        accumulate(qi - 1)
        scores(qi, True)

    accumulate(qi)
    o_ref[...] = (acc_ref[...] / l_ref[...]).T.astype(BF16)


def _fox_attn(first_block, qk, vt, f_terms, *, batch, seq, heads, tq):
    n = qk.shape[0]
    d = heads * FOX_HEAD_DIM
    nq = seq // tq
    f_cols = f_terms.shape[1]
    assert seq % tq == 0 and vt.shape == (n // tq, d, tq) and first_block.shape == (batch * heads * nq,)
    blocks = 2 * tq * FOX_HEAD_DIM * 2 + 2 * seq * FOX_HEAD_DIM * 2 + seq * f_cols * 2
    temps = seq * 2 * FOX_HEAD_DIM * 2 + tq * FOX_HEAD_DIM * 4 + 6 * tq * tq * 4
    return pl.pallas_call(
        functools.partial(_fox_attn_kernel, tk=tq),
        out_shape=jax.ShapeDtypeStruct((n, d), BF16),
        grid_spec=pltpu.PrefetchScalarGridSpec(
            num_scalar_prefetch=1,
            grid=(batch, heads, nq),
            in_specs=[
                pl.BlockSpec((tq, FOX_HEAD_DIM), lambda b, h, qi, first: (b * nq + qi, h)),
                pl.BlockSpec((seq, FOX_HEAD_DIM), lambda b, h, qi, first: (b, heads + h)),
                pl.BlockSpec((nq, FOX_HEAD_DIM, tq), lambda b, h, qi, first: (b, h, 0)),
                pl.BlockSpec((seq, f_cols), lambda b, h, qi, first: (b, 0)),
            ],
            out_specs=pl.BlockSpec((tq, FOX_HEAD_DIM), lambda b, h, qi, first: (b * nq + qi, h)),
            scratch_shapes=[pltpu.VMEM((seq, 2 * FOX_HEAD_DIM), BF16), pltpu.VMEM((tq, 2 * FOX_HEAD_DIM), BF16),
                            pltpu.VMEM((tq, tq), F32), pltpu.VMEM((1, tq), F32), pltpu.VMEM((1, tq), F32),
                            pltpu.VMEM((1, tq), F32), pltpu.VMEM((FOX_HEAD_DIM, tq), F32)]),
        compiler_params=pltpu.CompilerParams(
            dimension_semantics=("arbitrary", "arbitrary", "arbitrary"),
            vmem_limit_bytes=_vmem_limit(blocks, temps)),
        name="fox_attn",
    )(first_block, qk, qk, vt, f_terms)


def _fox_out_kernel(o_ref, w_ref, x_ref, y_ref):
    y_ref[...] = x_ref[...] + _dot(o_ref[...], w_ref[...].astype(BF16))


def _fox_out(o, w_out, mixer, x, *, tm, tn=512):
    n, d = x.shape
    assert n % tm == 0 and d % tn == 0
    blocks = tm * d * 2 + d * tn * 4 + 2 * tm * tn * 4
    temps = d * tn * 2 + tm * tn * 4
    return pl.pallas_call(
        _fox_out_kernel,
        out_shape=jax.ShapeDtypeStruct((n, d), F32),
        grid=(n // tm, d // tn),
        in_specs=[
            pl.BlockSpec((tm, d), lambda i, j: (i, 0)),
            pl.BlockSpec((None, d, tn), lambda i, j: (mixer, 0, j)),
            pl.BlockSpec((tm, tn), lambda i, j: (i, j)),
        ],
        out_specs=pl.BlockSpec((tm, tn), lambda i, j: (i, j)),
        compiler_params=pltpu.CompilerParams(
            dimension_semantics=("arbitrary", "arbitrary"),
            vmem_limit_bytes=_vmem_limit(blocks, temps)),
        name="fox_out",
    )(o, w_out, x)


def _fox(x, gains, layer, w_in, w_vt, w_f, b_f, q_gains, k_gains, w_out, mixer, *, batch, seq, tm, tq):
    heads = x.shape[1] // FOX_HEAD_DIM
    qk, vt, f_terms, first = _fox_proj(x, gains, layer, w_in, w_vt, w_f, b_f, q_gains, k_gains, mixer,
                                       seq=seq, tm=tm, tk=tq)
    first = first[:, :tm // tq, :heads].reshape(batch, seq // tq, heads).transpose(0, 2, 1).reshape(-1)
    o = _fox_attn(first, qk, vt, f_terms, batch=batch, seq=seq, heads=heads, tq=tq)
    return _fox_out(o, w_out, mixer, x, tm=tm)


def kernel(x, ffn1_norm, ffn1_w_gate, ffn1_w_up, ffn1_w_down, mix_norm, pool_w, pool_scale, fox_w_in, fox_b_f,
           fox_q_gain, fox_k_gain, fox_w_out, ffn2_norm, ffn2_w_gate, ffn2_w_up, ffn2_w_down):
    batch, seq, d = x.shape
    depth = ffn1_norm.shape[0]
    n_mixers = 2
    heads = d // FOX_HEAD_DIM
    tm = min(1024, seq)
    tq = min(512, seq)
    assert heads <= V7X_LANES

    w_vt = jnp.swapaxes(fox_w_in[:, :, 2 * d:3 * d], 1, 2)
    w_f = jnp.pad(fox_w_in[:, :, 3 * d:], ((0, 0), (0, 0), (0, V7X_LANES - heads))).astype(BF16)
    b_f = jnp.pad(fox_b_f, ((0, 0), (0, V7X_LANES - heads))).reshape(-1, 1, V7X_LANES)

    h = x.reshape(batch * seq, d)
    for i in range(depth):
        h = _ffn(h, ffn1_norm, ffn1_w_gate, ffn1_w_up, ffn1_w_down, i, tm=tm)
        j = i // n_mixers
        if i % n_mixers == 0:
            h = _pool(h, mix_norm, i, pool_w, pool_scale, j, seq=seq, tm=min(512, seq))
        else:
            h = _fox(h, mix_norm, i, fox_w_in, w_vt, w_f, b_f, fox_q_gain, fox_k_gain, fox_w_out, j,
                     batch=batch, seq=seq, tm=tm, tq=tq)
        h = _ffn(h, ffn2_norm, ffn2_w_gate, ffn2_w_up, ffn2_w_down, i, tm=tm)
    return h.reshape(batch, seq, d)
```

```python
import functools
import math

import jax
import jax.numpy as jnp
from jax import lax
from jax.experimental import pallas as pl
from jax.experimental.pallas import tpu as pltpu

RMS_EPS = 1e-6
FFN_RESIDUAL_WEIGHT = 0.5
POOL_WINDOWS = (2, 4, 8, 16)
FOX_HEAD_DIM = 128
NEG_LARGE = -1e30
LOG2E = math.log2(math.e)

SKIP_LOG2_MARGIN = 160.0
QK_BOUND_SLACK = 1.01

V7X_LANES = 128
V7X_SUBLANES = 8
V7X_VMEM_BYTES = 64 * 1024 * 1024
POOL_HALO = 32
POOL_FIRST_ROW = 8
F_TERMS = 3

BF16 = jnp.bfloat16
F32 = jnp.float32


def _vmem_limit(block_bytes, temp_bytes):
    return min(2 * block_bytes + temp_bytes + (4 << 20), V7X_VMEM_BYTES - (2 << 20))


def _rms_norm(x, gain):
    ms = jnp.mean(x * x, axis=-1, keepdims=True)
    return x * lax.rsqrt(ms + RMS_EPS) * gain


def _dot(a, b):
    return jnp.dot(a, b, preferred_element_type=F32)


def _split_bf16(x):
    terms = []
    for _ in range(F_TERMS):
        t = x.astype(BF16)
        terms.append(t)
        x = x - t.astype(F32)
    return terms


def _ffn_kernel(x_ref, g_ref, wg_ref, wu_ref, wd_ref, o_ref, h_ref, *, first_step_chunk):
    j = pl.program_id(1)
    last = pl.num_programs(1) - 1
    tm = x_ref.shape[0]

    def weights():
        return wg_ref[...].astype(BF16), wu_ref[...].astype(BF16), wd_ref[...].astype(BF16)

    def partial_out(h, wg, wu, wd):
        gate = _dot(h, wg)
        up = _dot(h, wu)
        return _dot((gate * jax.nn.sigmoid(gate) * up).astype(BF16), wd)

    @pl.when(j == 0)
    def _():
        w = weights()
        for r0 in range(0, tm, first_step_chunk):
            rows = slice(r0, r0 + first_step_chunk)
            h = _rms_norm(x_ref[rows, :], g_ref[...]).astype(BF16)
            h_ref[rows, :] = h
            o_ref[rows, :] = partial_out(h, *w)

    @pl.when((j > 0) & (j < last))
    def _():
        o_ref[...] += partial_out(h_ref[...], *weights())

    @pl.when(j == last)
    def _():
        o_ref[...] = x_ref[...] + FFN_RESIDUAL_WEIGHT * (o_ref[...] + partial_out(h_ref[...], *weights()))


def _ffn(x, gains, w_gate, w_up, w_down, layer, *, tm, tf=256):
    n, d = x.shape
    f = w_gate.shape[2]
    assert n % tm == 0 and f % tf == 0 and f // tf >= 2
    blocks = 2 * tm * d * 4 + 3 * d * tf * 4 + d * 4
    temps = tm * d * 2 + 3 * d * tf * 2 + 3 * tm * tf * 4
    return pl.pallas_call(
        functools.partial(_ffn_kernel, first_step_chunk=min(256, tm)),
        out_shape=jax.ShapeDtypeStruct((n, d), F32),
        grid=(n // tm, f // tf),
        in_specs=[
            pl.BlockSpec((tm, d), lambda i, j: (i, 0)),
            pl.BlockSpec((None, 1, d), lambda i, j: (layer, 0, 0)),
            pl.BlockSpec((None, d, tf), lambda i, j: (layer, 0, j)),
            pl.BlockSpec((None, d, tf), lambda i, j: (layer, 0, j)),
            pl.BlockSpec((None, tf, d), lambda i, j: (layer, j, 0)),
        ],
        out_specs=pl.BlockSpec((tm, d), lambda i, j: (i, 0)),
        scratch_shapes=[pltpu.VMEM((tm, d), BF16)],
        compiler_params=pltpu.CompilerParams(
            dimension_semantics=("arbitrary", "arbitrary"),
            vmem_limit_bytes=_vmem_limit(blocks, temps)),
        name="ffn",
    )(x, gains.reshape(gains.shape[0], 1, d), w_gate, w_up, w_down)


def _pool_kernel(x_ref, xh_ref, g_ref, w_ref, s_ref, o_ref, h_ref, p_ref, q_ref, *, tiles_per_batch):
    tm, d = x_ref.shape
    group = d // len(POOL_WINDOWS)
    rows = tm + POOL_HALO
    tile_in_batch = lax.rem(pl.program_id(0), tiles_per_batch)

    gain = g_ref[...]
    halo = _rms_norm(xh_ref[...], gain)
    h_ref[0:POOL_HALO, :] = jnp.where(tile_in_batch == 0, 0.0, halo)
    h_ref[POOL_HALO:, :] = _rms_norm(x_ref[...], gain)

    src = h_ref
    first_exact_row = POOL_FIRST_ROW
    for step in range(len(POOL_WINDOWS)):
        shift = 1 << step
        assert POOL_WINDOWS[step] == 2 * shift and shift <= POOL_FIRST_ROW
        first_exact_row += shift if step else 0
        dst = p_ref if step % 2 == 0 else q_ref
        c0 = step * group
        dst[POOL_FIRST_ROW:, c0:] = (src[POOL_FIRST_ROW:, c0:]
                                     + src[POOL_FIRST_ROW - shift:rows - shift, c0:])
        src = dst
    assert first_exact_row <= POOL_HALO

    pos = tile_in_batch * tm + lax.broadcasted_iota(jnp.int32, (tm, 1), 0)
    for gi, window in enumerate(POOL_WINDOWS):
        cols = slice(gi * group, (gi + 1) * group)
        sums = (p_ref if gi % 2 == 0 else q_ref)[POOL_HALO:, cols]
        count = jnp.minimum(pos + 1, window).astype(F32)
        diff = (sums / count - h_ref[POOL_HALO:, cols]).astype(BF16)
        y = _dot(diff, w_ref[gi].astype(BF16))
        o_ref[:, cols] = x_ref[:, cols] + y * s_ref[:, cols]


def _pool(x, gains, layer, w_groups, scales, mixer, *, seq, tm=512):
    n, d = x.shape
    _, groups, gsz, _ = w_groups.shape
    assert seq % tm == 0 and tm % POOL_HALO == 0 and groups * gsz == d
    halo_blocks = tm // POOL_HALO
    blocks = 2 * tm * d * 4 + POOL_HALO * d * 4 + groups * gsz * gsz * 4 + 2 * d * 4
    temps = 3 * (tm + POOL_HALO) * d * 4 + 2 * tm * d * 4
    return pl.pallas_call(
        functools.partial(_pool_kernel, tiles_per_batch=seq // tm),
        out_shape=jax.ShapeDtypeStruct((n, d), F32),
        grid=(n // tm,),
        in_specs=[
            pl.BlockSpec((tm, d), lambda i: (i, 0)),
            pl.BlockSpec((POOL_HALO, d), lambda i: (jnp.maximum(i * halo_blocks - 1, 0), 0)),
            pl.BlockSpec((None, 1, d), lambda i: (layer, 0, 0)),
            pl.BlockSpec((None, groups, gsz, gsz), lambda i: (mixer, 0, 0, 0)),
            pl.BlockSpec((None, 1, d), lambda i: (mixer, 0, 0)),
        ],
        out_specs=pl.BlockSpec((tm, d), lambda i: (i, 0)),
        scratch_shapes=[pltpu.VMEM((tm + POOL_HALO, d), F32)] * 3,
        compiler_params=pltpu.CompilerParams(
            dimension_semantics=("arbitrary",),
            vmem_limit_bytes=_vmem_limit(blocks, temps)),
        name="pool",
    )(x, x, gains.reshape(gains.shape[0], 1, d), w_groups, scales.reshape(scales.shape[0], 1, d))


def _log_sigmoid(z):
    return jnp.minimum(z, 0.0) - jnp.log1p(jnp.exp(-jnp.abs(z)))


def _fox_proj_kernel(x_ref, g_ref, w_ref, wf_ref, bf_ref, gq_ref, gk_ref, qkv_ref, f_ref, ks_ref,
                     h_ref, carry_ref, fmin_ref, *, tiles_per_batch, qk_tiles, tk, chunk):
    i = pl.program_id(0)
    j = pl.program_id(1)
    tm, tn = qkv_ref.shape
    blocks_per_tile = tm // tk

    @pl.when(j == 0)
    def _():
        h_ref[...] = _rms_norm(x_ref[...], g_ref[...]).astype(BF16)
        tile_in_batch = lax.rem(i, tiles_per_batch)

        @pl.when(tile_in_batch == 0)
        def _():
            carry_ref[...] = jnp.zeros_like(carry_ref)
            fmin_ref[...] = jnp.zeros_like(fmin_ref)

        r = lax.broadcasted_iota(jnp.int32, (chunk, chunk), 0)
        c = lax.broadcasted_iota(jnp.int32, (chunk, chunk), 1)
        tri = (c <= r).astype(BF16)
        carry = carry_ref[...]
        block_max = [None] * blocks_per_tile
        block_min = [None] * blocks_per_tile
        for c0 in range(0, tm, chunk):
            logit = _dot(h_ref[c0:c0 + chunk, :], wf_ref[...]) + bf_ref[...]
            prefix = sum(_dot(tri, t) for t in _split_bf16(_log_sigmoid(logit)))
            total = prefix + carry
            carry = total[chunk - 1:chunk, :]
            f2 = total * LOG2E
            for t, term in enumerate(_split_bf16(f2)):
                f_ref[c0:c0 + chunk, t * V7X_LANES:(t + 1) * V7X_LANES] = term
            blk = c0 // tk
            hi = jnp.max(f2, axis=0, keepdims=True)
            lo = jnp.min(f2, axis=0, keepdims=True)
            block_max[blk] = hi if block_max[blk] is None else jnp.maximum(block_max[blk], hi)
            block_min[blk] = lo if block_min[blk] is None else jnp.minimum(block_min[blk], lo)
        carry_ref[...] = carry

        qk_bound = (LOG2E * math.sqrt(FOX_HEAD_DIM) * QK_BOUND_SLACK
                    * jnp.max(jnp.abs(gq_ref[...]), axis=-1, keepdims=True)
                    * jnp.max(jnp.abs(gk_ref[...]), axis=-1, keepdims=True))
        n_blocks = fmin_ref.shape[0]
        row = lax.broadcasted_iota(jnp.int32, fmin_ref.shape, 0)
        ks_ref[...] = jnp.zeros_like(ks_ref)
        for blk in range(blocks_per_tile):
            qb = tile_in_batch * blocks_per_tile + blk
            fmin_ref[pl.ds(qb, 1), :] = block_min[blk]
            gap = 2.0 * qk_bound + block_max[blk] - fmin_ref[...]
            needed = (gap > -SKIP_LOG2_MARGIN) | (row >= qb)
            ks_ref[blk:blk + 1, :] = jnp.min(jnp.where(needed, row, n_blocks), axis=0, keepdims=True)

    acc = _dot(h_ref[...], w_ref[...].astype(BF16))

    @pl.when(j < qk_tiles)
    def _():
        is_q = j < qk_tiles // 2
        gain = jnp.where(is_q, gq_ref[...], gk_ref[...])
        post = jnp.where(is_q, LOG2E / math.sqrt(FOX_HEAD_DIM), 1.0)
        for c0 in range(0, tn, FOX_HEAD_DIM):
            y = _rms_norm(acc[:, c0:c0 + FOX_HEAD_DIM], gain) * post
            qkv_ref[:, c0:c0 + FOX_HEAD_DIM] = y.astype(BF16)

    @pl.when(j >= qk_tiles)
    def _():
        qkv_ref[...] = acc.astype(BF16)


def _fox_proj(x, gains, layer, w_in, w_f, b_f, q_gains, k_gains, mixer, *, seq, tm, tk, tn=512, chunk=256):
    n, d = x.shape
    assert seq % tm == 0 and tm % tk == 0 and tk % chunk == 0 and d % tn == 0 and tn % FOX_HEAD_DIM == 0
    assert tm // tk <= V7X_SUBLANES
    qk_tiles = 2 * d // tn
    f_cols = F_TERMS * V7X_LANES
    blocks = tm * d * 4 + d * tn * 4 + d * V7X_LANES * 2 + tm * tn * 2 + tm * f_cols * 2
    temps = tm * d * 2 + d * tn * 2 + 3 * tm * tn * 4 + 4 * chunk * chunk * 4
    return pl.pallas_call(
        functools.partial(_fox_proj_kernel, tiles_per_batch=seq // tm, qk_tiles=qk_tiles, tk=tk, chunk=chunk),
        out_shape=(jax.ShapeDtypeStruct((n, 3 * d), BF16),
                   jax.ShapeDtypeStruct((n, f_cols), BF16),
                   jax.ShapeDtypeStruct((n // tm, V7X_SUBLANES, V7X_LANES), jnp.int32)),
        grid=(n // tm, 3 * d // tn),
        in_specs=[
            pl.BlockSpec((tm, d), lambda i, j: (i, 0)),
            pl.BlockSpec((None, 1, d), lambda i, j: (layer, 0, 0)),
            pl.BlockSpec((None, d, tn), lambda i, j: (mixer, 0, j)),
            pl.BlockSpec((None, d, V7X_LANES), lambda i, j: (mixer, 0, 0)),
            pl.BlockSpec((None, 1, V7X_LANES), lambda i, j: (mixer, 0, 0)),
            pl.BlockSpec((None, 1, FOX_HEAD_DIM), lambda i, j: (mixer, 0, 0)),
            pl.BlockSpec((None, 1, FOX_HEAD_DIM), lambda i, j: (mixer, 0, 0)),
        ],
        out_specs=(pl.BlockSpec((tm, tn), lambda i, j: (i, j)),
                   pl.BlockSpec((tm, f_cols), lambda i, j: (i, 0)),
                   pl.BlockSpec((None, V7X_SUBLANES, V7X_LANES), lambda i, j: (i, 0, 0))),
        scratch_shapes=[pltpu.VMEM((tm, d), BF16), pltpu.VMEM((1, V7X_LANES), F32),
                        pltpu.VMEM((seq // tk, V7X_LANES), F32)],
        compiler_params=pltpu.CompilerParams(
            dimension_semantics=("arbitrary", "arbitrary"),
            vmem_limit_bytes=_vmem_limit(blocks, temps)),
        name="fox_proj",
    )(x, gains.reshape(gains.shape[0], 1, d), w_in, w_f, b_f,
      q_gains.reshape(q_gains.shape[0], 1, -1), k_gains.reshape(k_gains.shape[0], 1, -1))


def _fox_attn_kernel(first_ref, q_ref, k_ref, v_ref, f_ref, o_ref, kaug_ref, qaug_ref, s_ref, cmax_ref, m_ref,
                     l_ref, acc_ref, *, tq, tk):
    head = pl.program_id(1)
    seq = k_ref.shape[0]
    nq = seq // tq
    assert tq == tk
    table_base = (pl.program_id(0) * pl.num_programs(1) + head) * nq

    r = lax.broadcasted_iota(jnp.int32, (F_TERMS * V7X_LANES, V7X_LANES), 0)
    c = lax.broadcasted_iota(jnp.int32, (F_TERMS * V7X_LANES, V7X_LANES), 1)
    sel_q = (r == c * V7X_LANES + head).astype(BF16)
    sel_k = -(r == (c - F_TERMS) * V7X_LANES + head).astype(BF16)
    lane = lax.broadcasted_iota(jnp.int32, (1, V7X_LANES), 1)
    ones_q = ((lane >= F_TERMS) & (lane < 2 * F_TERMS)).astype(F32)
    ones_k = (lane < F_TERMS).astype(F32)

    def fill(kb, carry):
        rows = pl.ds(pl.multiple_of(kb * tk, tk), tk)
        kaug_ref[rows, 0:FOX_HEAD_DIM] = k_ref[rows, :]
        kaug_ref[rows, FOX_HEAD_DIM:] = (_dot(f_ref[rows, :], sel_k) + ones_k).astype(BF16)
        return carry
    lax.fori_loop(0, seq // tk, fill, 0)

    def scores(kb, masked):
        rows = pl.ds(pl.multiple_of(kb * tk, tk), tk)
        s = lax.dot_general(kaug_ref[rows, :], qaug_ref[...], (((1,), (1,)), ((), ())),
                            preferred_element_type=F32)
        if masked:
            key = lax.broadcasted_iota(jnp.int32, (tk, tq), 0)
            qry = lax.broadcasted_iota(jnp.int32, (tk, tq), 1)
            s = jnp.where(key <= qry, s, NEG_LARGE)
        cmax_ref[...] = jnp.max(s, axis=0, keepdims=True)
        s_ref[...] = s

    def accumulate(kb):
        rows = pl.ds(pl.multiple_of(kb * tk, tk), tk)
        m_old = m_ref[...]
        m_new = jnp.maximum(m_old, cmax_ref[...])
        alpha = jnp.exp2(m_old - m_new)
        p = jnp.exp2(s_ref[...] - m_new)
        l_ref[...] = alpha * l_ref[...] + jnp.sum(p, axis=0, keepdims=True)
        pv = lax.dot_general(v_ref[rows, :], p.astype(BF16), (((0,), (0,)), ((), ())),
                             preferred_element_type=F32)
        acc_ref[...] = alpha * acc_ref[...] + pv
        m_ref[...] = m_new

    def query_block(qi, carry):
        first = first_ref[table_base + qi]
        q_rows = pl.ds(pl.multiple_of(qi * tq, tq), tq)
        qaug_ref[:, 0:FOX_HEAD_DIM] = q_ref[q_rows, :]
        qaug_ref[:, FOX_HEAD_DIM:] = (_dot(f_ref[q_rows, :], sel_q) + ones_q).astype(BF16)
        m_ref[...] = jnp.full_like(m_ref, NEG_LARGE)
        l_ref[...] = jnp.zeros_like(l_ref)
        acc_ref[...] = jnp.zeros_like(acc_ref)

        @pl.when(first >= qi)
        def _():
            scores(qi, True)

        @pl.when(first < qi)
        def _():
            scores(first, False)

            def body(kb, c):
                accumulate(kb)
                scores(kb + 1, False)
                return c
            lax.fori_loop(first, qi - 1, body, 0)
            accumulate(qi - 1)
            scores(qi, True)

        accumulate(qi)
        o_ref[q_rows, :] = (acc_ref[...] / l_ref[...]).T.astype(BF16)
        return carry

    lax.fori_loop(0, nq, query_block, 0)


def _fox_attn(first_block, qkv, f_terms, *, batch, seq, heads, tq):
    n = qkv.shape[0]
    d = heads * FOX_HEAD_DIM
    nq = seq // tq
    f_cols = f_terms.shape[1]
    assert seq % tq == 0 and qkv.shape == (n, 3 * d) and first_block.shape == (batch * heads * nq,)
    blocks = 4 * seq * FOX_HEAD_DIM * 2 + seq * f_cols * 2
    temps = (seq + tq) * 2 * FOX_HEAD_DIM * 2 + tq * FOX_HEAD_DIM * 4 + 6 * tq * tq * 4
    return pl.pallas_call(
        functools.partial(_fox_attn_kernel, tq=tq, tk=tq),
        out_shape=jax.ShapeDtypeStruct((n, d), BF16),
        grid_spec=pltpu.PrefetchScalarGridSpec(
            num_scalar_prefetch=1,
            grid=(batch, heads),
            in_specs=[
                pl.BlockSpec((seq, FOX_HEAD_DIM), lambda b, h, first: (b, h)),
                pl.BlockSpec((seq, FOX_HEAD_DIM), lambda b, h, first: (b, heads + h)),
                pl.BlockSpec((seq, FOX_HEAD_DIM), lambda b, h, first: (b, 2 * heads + h)),
                pl.BlockSpec((seq, f_cols), lambda b, h, first: (b, 0)),
            ],
            out_specs=pl.BlockSpec((seq, FOX_HEAD_DIM), lambda b, h, first: (b, h)),
            scratch_shapes=[pltpu.VMEM((seq, 2 * FOX_HEAD_DIM), BF16), pltpu.VMEM((tq, 2 * FOX_HEAD_DIM), BF16),
                            pltpu.VMEM((tq, tq), F32), pltpu.VMEM((1, tq), F32), pltpu.VMEM((1, tq), F32),
                            pltpu.VMEM((1, tq), F32), pltpu.VMEM((FOX_HEAD_DIM, tq), F32)]),
        compiler_params=pltpu.CompilerParams(
            dimension_semantics=("arbitrary", "arbitrary"),
            vmem_limit_bytes=_vmem_limit(blocks, temps)),
        name="fox_attn",
    )(first_block, qkv, qkv, qkv, f_terms)


def _fox_out_kernel(o_ref, w_ref, x_ref, y_ref):
    y_ref[...] = x_ref[...] + _dot(o_ref[...], w_ref[...].astype(BF16))


def _fox_out(o, w_out, mixer, x, *, tm, tn):
    n, d = x.shape
    assert n % tm == 0 and d % tn == 0
    blocks = tm * d * 2 + d * tn * 4 + 2 * tm * tn * 4
    temps = d * tn * 2 + tm * tn * 4
    return pl.pallas_call(
        _fox_out_kernel,
        out_shape=jax.ShapeDtypeStruct((n, d), F32),
        grid=(n // tm, d // tn),
        in_specs=[
            pl.BlockSpec((tm, d), lambda i, j: (i, 0)),
            pl.BlockSpec((None, d, tn), lambda i, j: (mixer, 0, j)),
            pl.BlockSpec((tm, tn), lambda i, j: (i, j)),
        ],
        out_specs=pl.BlockSpec((tm, tn), lambda i, j: (i, j)),
        compiler_params=pltpu.CompilerParams(
            dimension_semantics=("arbitrary", "arbitrary"),
            vmem_limit_bytes=_vmem_limit(blocks, temps)),
        name="fox_out",
    )(o, w_out, x)


def _fox(x, gains, layer, w_in, w_f, b_f, q_gains, k_gains, w_out, mixer, *, batch, seq, tm, tq):
    heads = x.shape[1] // FOX_HEAD_DIM
    qkv, f_terms, first = _fox_proj(x, gains, layer, w_in, w_f, b_f, q_gains, k_gains, mixer, seq=seq, tm=tm, tk=tq)
    first = first[:, :tm // tq, :heads].reshape(batch, seq // tq, heads).transpose(0, 2, 1).reshape(-1)
    o = _fox_attn(first, qkv, f_terms, batch=batch, seq=seq, heads=heads, tq=tq)
    return _fox_out(o, w_out, mixer, x, tm=tm, tn=min(1024, x.shape[1]))


def kernel(x, ffn1_norm, ffn1_w_gate, ffn1_w_up, ffn1_w_down, mix_norm, pool_w, pool_scale, fox_w_in, fox_b_f,
           fox_q_gain, fox_k_gain, fox_w_out, ffn2_norm, ffn2_w_gate, ffn2_w_up, ffn2_w_down):
    batch, seq, d = x.shape
    depth = ffn1_norm.shape[0]
    n_mixers = 2
    heads = d // FOX_HEAD_DIM
    tm = min(1024, seq)
    tq = min(512, seq)
    assert heads <= V7X_LANES

    w_f = jnp.pad(fox_w_in[:, :, 3 * d:], ((0, 0), (0, 0), (0, V7X_LANES - heads))).astype(BF16)
    b_f = jnp.pad(fox_b_f, ((0, 0), (0, V7X_LANES - heads))).reshape(-1, 1, V7X_LANES)

    h = x.reshape(batch * seq, d)
    for i in range(depth):
        h = _ffn(h, ffn1_norm, ffn1_w_gate, ffn1_w_up, ffn1_w_down, i, tm=tm)
        j = i // n_mixers
        if i % n_mixers == 0:
            h = _pool(h, mix_norm, i, pool_w, pool_scale, j, seq=seq, tm=min(512, seq))
        else:
            h = _fox(h, mix_norm, i, fox_w_in, w_f, b_f, fox_q_gain, fox_k_gain, fox_w_out, j,
                     batch=batch, seq=seq, tm=tm, tq=tq)
        h = _ffn(h, ffn2_norm, ffn2_w_gate, ffn2_w_up, ffn2_w_down, i, tm=tm)
    return h.reshape(batch, seq, d)
```

```python
import functools
import math

import jax
import jax.numpy as jnp
from jax import lax
from jax.experimental import pallas as pl
from jax.experimental.pallas import tpu as pltpu

RMS_EPS = 1e-6
FFN_RESIDUAL_WEIGHT = 0.5
POOL_WINDOWS = (2, 4, 8, 16)
FOX_HEAD_DIM = 128
NEG_LARGE = -1e30
LOG2E = math.log2(math.e)

SKIP_LOG2_MARGIN = 160.0
QK_BOUND_SLACK = 1.01

V7X_LANES = 128
V7X_SUBLANES = 8
V7X_VMEM_BYTES = 64 * 1024 * 1024
POOL_HALO = 32
POOL_FIRST_ROW = 8
F_TERMS = 3
FFN_WEIGHT_SLOTS = 3

BF16 = jnp.bfloat16
F32 = jnp.float32


def _vmem_limit(block_bytes, temp_bytes):
    return min(2 * block_bytes + temp_bytes + (4 << 20), V7X_VMEM_BYTES - (2 << 20))


def _rms_norm(x, gain):
    ms = jnp.mean(x * x, axis=-1, keepdims=True)
    return x * lax.rsqrt(ms + RMS_EPS) * gain


def _dot(a, b):
    return jnp.dot(a, b, preferred_element_type=F32)


def _dot_nt(a, b):
    return lax.dot_general(a, b, (((1,), (1,)), ((), ())), preferred_element_type=F32)


def _split_bf16(x):
    terms = []
    for _ in range(F_TERMS):
        t = x.astype(BF16)
        terms.append(t)
        x = x - t.astype(F32)
    return terms


def _ffn_kernel(x_ref, g_ref, wg_hbm, wu_hbm, wd_hbm, o_ref, h_ref, wg_buf, wu_buf, wd_buf, sems,
                *, layer, first_step_chunk):
    i = pl.program_id(0)
    j = pl.program_id(1)
    n_j = pl.num_programs(1)
    last = n_j - 1
    tm = x_ref.shape[0]
    slots, _, tf = wg_buf.shape
    step = i * n_j + j
    n_steps = pl.num_programs(0) * n_j

    def tile_copies(s):
        col = pl.multiple_of(lax.rem(s, n_j) * tf, tf)
        slot = lax.rem(s, slots)
        return (pltpu.make_async_copy(wg_hbm.at[layer, :, pl.ds(col, tf)], wg_buf.at[slot], sems.at[0, slot]),
                pltpu.make_async_copy(wu_hbm.at[layer, :, pl.ds(col, tf)], wu_buf.at[slot], sems.at[1, slot]),
                pltpu.make_async_copy(wd_hbm.at[layer, pl.ds(col, tf), :], wd_buf.at[slot], sems.at[2, slot]))

    @pl.when(step == 0)
    def _():
        for s in range(slots - 1):
            for copy in tile_copies(jnp.int32(s)):
                copy.start()

    @pl.when(step + (slots - 1) < n_steps)
    def _():
        for copy in tile_copies(step + (slots - 1)):
            copy.start()

    for copy in tile_copies(step):
        copy.wait()
    slot = lax.rem(step, slots)

    def weights():
        return wg_buf[slot].astype(BF16), wu_buf[slot].astype(BF16), wd_buf[slot].astype(BF16)

    def partial_out(h, wg, wu, wd):
        gate = _dot(h, wg)
        up = _dot(h, wu)
        return _dot((gate * jax.nn.sigmoid(gate) * up).astype(BF16), wd)

    @pl.when(j == 0)
    def _():
        w = weights()
        for r0 in range(0, tm, first_step_chunk):
            rows = slice(r0, r0 + first_step_chunk)
            h = _rms_norm(x_ref[rows, :], g_ref[...]).astype(BF16)
            h_ref[rows, :] = h
            o_ref[rows, :] = partial_out(h, *w)

    @pl.when((j > 0) & (j < last))
    def _():
        o_ref[...] += partial_out(h_ref[...], *weights())

    @pl.when(j == last)
    def _():
        o_ref[...] = x_ref[...] + FFN_RESIDUAL_WEIGHT * (o_ref[...] + partial_out(h_ref[...], *weights()))


def _ffn(x, gains, w_gate, w_up, w_down, layer, *, tm, tf=256):
    n, d = x.shape
    f = w_gate.shape[2]
    assert n % tm == 0 and f % tf == 0 and f // tf >= 2
    slots = FFN_WEIGHT_SLOTS
    assert (n // tm) * (f // tf) >= slots - 1
    blocks = 2 * tm * d * 4 + d * 4
    temps = tm * d * 2 + slots * 3 * d * tf * 4 + 3 * d * tf * 2 + 3 * tm * tf * 4
    return pl.pallas_call(
        functools.partial(_ffn_kernel, layer=layer, first_step_chunk=min(256, tm)),
        out_shape=jax.ShapeDtypeStruct((n, d), F32),
        grid=(n // tm, f // tf),
        in_specs=[
            pl.BlockSpec((tm, d), lambda i, j: (i, 0)),
            pl.BlockSpec((None, 1, d), lambda i, j: (layer, 0, 0)),
            pl.BlockSpec(memory_space=pl.ANY),
            pl.BlockSpec(memory_space=pl.ANY),
            pl.BlockSpec(memory_space=pl.ANY),
        ],
        out_specs=pl.BlockSpec((tm, d), lambda i, j: (i, 0)),
        scratch_shapes=[pltpu.VMEM((tm, d), BF16),
                        pltpu.VMEM((slots, d, tf), F32), pltpu.VMEM((slots, d, tf), F32),
                        pltpu.VMEM((slots, tf, d), F32), pltpu.SemaphoreType.DMA((3, slots))],
        compiler_params=pltpu.CompilerParams(
            dimension_semantics=("arbitrary", "arbitrary"),
            vmem_limit_bytes=_vmem_limit(blocks, temps)),
        name="ffn",
    )(x, gains.reshape(gains.shape[0], 1, d), w_gate, w_up, w_down)


def _pool_kernel(x_ref, xh_ref, g_ref, w_ref, s_ref, o_ref, h_ref, p_ref, q_ref, *, tiles_per_batch):
    tm, d = x_ref.shape
    group = d // len(POOL_WINDOWS)
    rows = tm + POOL_HALO
    tile_in_batch = lax.rem(pl.program_id(0), tiles_per_batch)

    gain = g_ref[...]
    halo = _rms_norm(xh_ref[...], gain)
    h_ref[0:POOL_HALO, :] = jnp.where(tile_in_batch == 0, 0.0, halo)
    h_ref[POOL_HALO:, :] = _rms_norm(x_ref[...], gain)

    src = h_ref
    first_exact_row = POOL_FIRST_ROW
    for step in range(len(POOL_WINDOWS)):
        shift = 1 << step
        assert POOL_WINDOWS[step] == 2 * shift and shift <= POOL_FIRST_ROW
        first_exact_row += shift if step else 0
        dst = p_ref if step % 2 == 0 else q_ref
        c0 = step * group
        dst[POOL_FIRST_ROW:, c0:] = (src[POOL_FIRST_ROW:, c0:]
                                     + src[POOL_FIRST_ROW - shift:rows - shift, c0:])
        src = dst
    assert first_exact_row <= POOL_HALO

    pos = tile_in_batch * tm + lax.broadcasted_iota(jnp.int32, (tm, 1), 0)
    for gi, window in enumerate(POOL_WINDOWS):
        cols = slice(gi * group, (gi + 1) * group)
        sums = (p_ref if gi % 2 == 0 else q_ref)[POOL_HALO:, cols]
        count = jnp.minimum(pos + 1, window).astype(F32)
        diff = (sums / count - h_ref[POOL_HALO:, cols]).astype(BF16)
        y = _dot(diff, w_ref[gi].astype(BF16))
        o_ref[:, cols] = x_ref[:, cols] + y * s_ref[:, cols]


def _pool(x, gains, layer, w_groups, scales, mixer, *, seq, tm=512):
    n, d = x.shape
    _, groups, gsz, _ = w_groups.shape
    assert seq % tm == 0 and tm % POOL_HALO == 0 and groups * gsz == d
    halo_blocks = tm // POOL_HALO
    blocks = 2 * tm * d * 4 + POOL_HALO * d * 4 + groups * gsz * gsz * 4 + 2 * d * 4
    temps = 3 * (tm + POOL_HALO) * d * 4 + 2 * tm * d * 4
    return pl.pallas_call(
        functools.partial(_pool_kernel, tiles_per_batch=seq // tm),
        out_shape=jax.ShapeDtypeStruct((n, d), F32),
        grid=(n // tm,),
        in_specs=[
            pl.BlockSpec((tm, d), lambda i: (i, 0)),
            pl.BlockSpec((POOL_HALO, d), lambda i: (jnp.maximum(i * halo_blocks - 1, 0), 0)),
            pl.BlockSpec((None, 1, d), lambda i: (layer, 0, 0)),
            pl.BlockSpec((None, groups, gsz, gsz), lambda i: (mixer, 0, 0, 0)),
            pl.BlockSpec((None, 1, d), lambda i: (mixer, 0, 0)),
        ],
        out_specs=pl.BlockSpec((tm, d), lambda i: (i, 0)),
        scratch_shapes=[pltpu.VMEM((tm + POOL_HALO, d), F32)] * 3,
        compiler_params=pltpu.CompilerParams(
            dimension_semantics=("arbitrary",),
            vmem_limit_bytes=_vmem_limit(blocks, temps)),
        name="pool",
    )(x, x, gains.reshape(gains.shape[0], 1, d), w_groups, scales.reshape(scales.shape[0], 1, d))


def _log_sigmoid(z):
    return jnp.minimum(z, 0.0) - jnp.log1p(jnp.exp(-jnp.abs(z)))


def _fox_proj_kernel(x_ref, g_ref, w_ref, wf_ref, bf_ref, gq_ref, gk_ref, qkv_ref, f_ref, ks_ref,
                     h_ref, carry_ref, fmin_ref, *, tiles_per_batch, qk_tiles, tk, chunk):
    i = pl.program_id(0)
    j = pl.program_id(1)
    tm, tn = qkv_ref.shape
    blocks_per_tile = tm // tk

    @pl.when(j == 0)
    def _():
        h_ref[...] = _rms_norm(x_ref[...], g_ref[...]).astype(BF16)
        tile_in_batch = lax.rem(i, tiles_per_batch)

        @pl.when(tile_in_batch == 0)
        def _():
            carry_ref[...] = jnp.zeros_like(carry_ref)
            fmin_ref[...] = jnp.zeros_like(fmin_ref)

        r = lax.broadcasted_iota(jnp.int32, (chunk, chunk), 0)
        c = lax.broadcasted_iota(jnp.int32, (chunk, chunk), 1)
        tri = (c <= r).astype(BF16)
        carry = carry_ref[...]
        block_max = [None] * blocks_per_tile
        block_min = [None] * blocks_per_tile
        for c0 in range(0, tm, chunk):
            logit = _dot_nt(h_ref[c0:c0 + chunk, :], wf_ref[...]) + bf_ref[...]
            prefix = sum(_dot(tri, t) for t in _split_bf16(_log_sigmoid(logit)))
            total = prefix + carry
            carry = total[chunk - 1:chunk, :]
            f2 = total * LOG2E
            for t, term in enumerate(_split_bf16(f2)):
                f_ref[c0:c0 + chunk, t * V7X_LANES:(t + 1) * V7X_LANES] = term
            blk = c0 // tk
            hi = jnp.max(f2, axis=0, keepdims=True)
            lo = jnp.min(f2, axis=0, keepdims=True)
            block_max[blk] = hi if block_max[blk] is None else jnp.maximum(block_max[blk], hi)
            block_min[blk] = lo if block_min[blk] is None else jnp.minimum(block_min[blk], lo)
        carry_ref[...] = carry

        qk_bound = (LOG2E * math.sqrt(FOX_HEAD_DIM) * QK_BOUND_SLACK
                    * jnp.max(jnp.abs(gq_ref[...]), axis=-1, keepdims=True)
                    * jnp.max(jnp.abs(gk_ref[...]), axis=-1, keepdims=True))
        n_blocks = fmin_ref.shape[0]
        row = lax.broadcasted_iota(jnp.int32, fmin_ref.shape, 0)
        ks_ref[...] = jnp.zeros_like(ks_ref)
        for blk in range(blocks_per_tile):
            qb = tile_in_batch * blocks_per_tile + blk
            fmin_ref[pl.ds(qb, 1), :] = block_min[blk]
            gap = 2.0 * qk_bound + block_max[blk] - fmin_ref[...]
            needed = (gap > -SKIP_LOG2_MARGIN) | (row >= qb)
            ks_ref[blk:blk + 1, :] = jnp.min(jnp.where(needed, row, n_blocks), axis=0, keepdims=True)

    acc = _dot_nt(h_ref[...], w_ref[...].astype(BF16))

    @pl.when(j < qk_tiles)
    def _():
        is_q = j < qk_tiles // 2
        gain = jnp.where(is_q, gq_ref[...], gk_ref[...])
        post = jnp.where(is_q, LOG2E / math.sqrt(FOX_HEAD_DIM), 1.0)
        for c0 in range(0, tn, FOX_HEAD_DIM):
            y = _rms_norm(acc[:, c0:c0 + FOX_HEAD_DIM], gain) * post
            qkv_ref[:, c0:c0 + FOX_HEAD_DIM] = y.astype(BF16)

    @pl.when(j >= qk_tiles)
    def _():
        qkv_ref[...] = acc.astype(BF16)


def _fox_proj(x, gains, layer, w_in_t, w_f_t, b_f, q_gains, k_gains, mixer, *, seq, tm, tk, tn=512, chunk=256):
    n, d = x.shape
    assert seq % tm == 0 and tm % tk == 0 and tk % chunk == 0 and d % tn == 0 and tn % FOX_HEAD_DIM == 0
    assert tm // tk <= V7X_SUBLANES
    qk_tiles = 2 * d // tn
    f_cols = F_TERMS * V7X_LANES
    blocks = tm * d * 4 + d * tn * 4 + d * V7X_LANES * 2 + tm * tn * 2 + tm * f_cols * 2
    temps = tm * d * 2 + d * tn * 2 + 3 * tm * tn * 4 + 4 * chunk * chunk * 4
    return pl.pallas_call(
        functools.partial(_fox_proj_kernel, tiles_per_batch=seq // tm, qk_tiles=qk_tiles, tk=tk, chunk=chunk),
        out_shape=(jax.ShapeDtypeStruct((n, 3 * d), BF16),
                   jax.ShapeDtypeStruct((n, f_cols), BF16),
                   jax.ShapeDtypeStruct((n // tm, V7X_SUBLANES, V7X_LANES), jnp.int32)),
        grid=(n // tm, 3 * d // tn),
        in_specs=[
            pl.BlockSpec((tm, d), lambda i, j: (i, 0)),
            pl.BlockSpec((None, 1, d), lambda i, j: (layer, 0, 0)),
            pl.BlockSpec((None, tn, d), lambda i, j: (mixer, j, 0)),
            pl.BlockSpec((None, V7X_LANES, d), lambda i, j: (mixer, 0, 0)),
            pl.BlockSpec((None, 1, V7X_LANES), lambda i, j: (mixer, 0, 0)),
            pl.BlockSpec((None, 1, FOX_HEAD_DIM), lambda i, j: (mixer, 0, 0)),
            pl.BlockSpec((None, 1, FOX_HEAD_DIM), lambda i, j: (mixer, 0, 0)),
        ],
        out_specs=(pl.BlockSpec((tm, tn), lambda i, j: (i, j)),
                   pl.BlockSpec((tm, f_cols), lambda i, j: (i, 0)),
                   pl.BlockSpec((None, V7X_SUBLANES, V7X_LANES), lambda i, j: (i, 0, 0))),
        scratch_shapes=[pltpu.VMEM((tm, d), BF16), pltpu.VMEM((1, V7X_LANES), F32),
                        pltpu.VMEM((seq // tk, V7X_LANES), F32)],
        compiler_params=pltpu.CompilerParams(
            dimension_semantics=("arbitrary", "arbitrary"),
            vmem_limit_bytes=_vmem_limit(blocks, temps)),
        name="fox_proj",
    )(x, gains.reshape(gains.shape[0], 1, d), w_in_t, w_f_t, b_f,
      q_gains.reshape(q_gains.shape[0], 1, -1), k_gains.reshape(k_gains.shape[0], 1, -1))


def _fox_attn_kernel(first_ref, q_ref, k_ref, v_ref, f_ref, o_ref, kaug_ref, qaug_ref, s_ref, cmax_ref, m_ref,
                     l_ref, acc_ref, *, tq, tk):
    head = pl.program_id(1)
    seq = k_ref.shape[0]
    nq = seq // tq
    assert tq == tk
    table_base = (pl.program_id(0) * pl.num_programs(1) + head) * nq

    r = lax.broadcasted_iota(jnp.int32, (F_TERMS * V7X_LANES, V7X_LANES), 0)
    c = lax.broadcasted_iota(jnp.int32, (F_TERMS * V7X_LANES, V7X_LANES), 1)
    sel_q = (r == c * V7X_LANES + head).astype(BF16)
    sel_k = -(r == (c - F_TERMS) * V7X_LANES + head).astype(BF16)
    lane = lax.broadcasted_iota(jnp.int32, (1, V7X_LANES), 1)
    ones_q = ((lane >= F_TERMS) & (lane < 2 * F_TERMS)).astype(F32)
    ones_k = (lane < F_TERMS).astype(F32)

    def fill(kb, carry):
        rows = pl.ds(pl.multiple_of(kb * tk, tk), tk)
        kaug_ref[rows, 0:FOX_HEAD_DIM] = k_ref[rows, :]
        kaug_ref[rows, FOX_HEAD_DIM:] = (_dot(f_ref[rows, :], sel_k) + ones_k).astype(BF16)
        return carry
    lax.fori_loop(0, seq // tk, fill, 0)

    def scores(kb, masked):
        rows = pl.ds(pl.multiple_of(kb * tk, tk), tk)
        s = lax.dot_general(kaug_ref[rows, :], qaug_ref[...], (((1,), (1,)), ((), ())),
                            preferred_element_type=F32)
        if masked:
            key = lax.broadcasted_iota(jnp.int32, (tk, tq), 0)
            qry = lax.broadcasted_iota(jnp.int32, (tk, tq), 1)
            s = jnp.where(key <= qry, s, NEG_LARGE)
        cmax_ref[...] = jnp.max(s, axis=0, keepdims=True)
        s_ref[...] = s

    def accumulate(kb):
        rows = pl.ds(pl.multiple_of(kb * tk, tk), tk)
        m_old = m_ref[...]
        m_new = jnp.maximum(m_old, cmax_ref[...])
        alpha = jnp.exp2(m_old - m_new)
        p = jnp.exp2(s_ref[...] - m_new)
        l_ref[...] = alpha * l_ref[...] + jnp.sum(p, axis=0, keepdims=True)
        pv = lax.dot_general(v_ref[rows, :], p.astype(BF16), (((0,), (0,)), ((), ())),
                             preferred_element_type=F32)
        acc_ref[...] = alpha * acc_ref[...] + pv
        m_ref[...] = m_new

    def query_block(qi, carry):
        first = first_ref[table_base + qi]
        q_rows = pl.ds(pl.multiple_of(qi * tq, tq), tq)
        qaug_ref[:, 0:FOX_HEAD_DIM] = q_ref[q_rows, :]
        qaug_ref[:, FOX_HEAD_DIM:] = (_dot(f_ref[q_rows, :], sel_q) + ones_q).astype(BF16)
        m_ref[...] = jnp.full_like(m_ref, NEG_LARGE)
        l_ref[...] = jnp.zeros_like(l_ref)
        acc_ref[...] = jnp.zeros_like(acc_ref)

        @pl.when(first >= qi)
        def _():
            scores(qi, True)

        @pl.when(first < qi)
        def _():
            scores(first, False)

            def body(kb, c):
                accumulate(kb)
                scores(kb + 1, False)
                return c
            lax.fori_loop(first, qi - 1, body, 0)
            accumulate(qi - 1)
            scores(qi, True)

        accumulate(qi)
        o_ref[q_rows, :] = (acc_ref[...] / l_ref[...]).T.astype(BF16)
        return carry

    lax.fori_loop(0, nq, query_block, 0)


def _fox_attn(first_block, qkv, f_terms, *, batch, seq, heads, tq):
    n = qkv.shape[0]
    d = heads * FOX_HEAD_DIM
    nq = seq // tq
    f_cols = f_terms.shape[1]
    assert seq % tq == 0 and qkv.shape == (n, 3 * d) and first_block.shape == (batch * heads * nq,)
    blocks = 4 * seq * FOX_HEAD_DIM * 2 + seq * f_cols * 2
    temps = (seq + tq) * 2 * FOX_HEAD_DIM * 2 + tq * FOX_HEAD_DIM * 4 + 6 * tq * tq * 4
    return pl.pallas_call(
        functools.partial(_fox_attn_kernel, tq=tq, tk=tq),
        out_shape=jax.ShapeDtypeStruct((n, d), BF16),
        grid_spec=pltpu.PrefetchScalarGridSpec(
            num_scalar_prefetch=1,
            grid=(batch, heads),
            in_specs=[
                pl.BlockSpec((seq, FOX_HEAD_DIM), lambda b, h, first: (b, h)),
                pl.BlockSpec((seq, FOX_HEAD_DIM), lambda b, h, first: (b, heads + h)),
                pl.BlockSpec((seq, FOX_HEAD_DIM), lambda b, h, first: (b, 2 * heads + h)),
                pl.BlockSpec((seq, f_cols), lambda b, h, first: (b, 0)),
            ],
            out_specs=pl.BlockSpec((seq, FOX_HEAD_DIM), lambda b, h, first: (b, h)),
            scratch_shapes=[pltpu.VMEM((seq, 2 * FOX_HEAD_DIM), BF16), pltpu.VMEM((tq, 2 * FOX_HEAD_DIM), BF16),
                            pltpu.VMEM((tq, tq), F32), pltpu.VMEM((1, tq), F32), pltpu.VMEM((1, tq), F32),
                            pltpu.VMEM((1, tq), F32), pltpu.VMEM((FOX_HEAD_DIM, tq), F32)]),
        compiler_params=pltpu.CompilerParams(
            dimension_semantics=("arbitrary", "arbitrary"),
            vmem_limit_bytes=_vmem_limit(blocks, temps)),
        name="fox_attn",
    )(first_block, qkv, qkv, qkv, f_terms)


def _fox_out_kernel(o_ref, w_ref, x_ref, y_ref):
    y_ref[...] = x_ref[...] + _dot(o_ref[...], w_ref[...].astype(BF16))


def _fox_out(o, w_out, mixer, x, *, tm, tn):
    n, d = x.shape
    assert n % tm == 0 and d % tn == 0
    blocks = tm * d * 2 + d * tn * 4 + 2 * tm * tn * 4
    temps = d * tn * 2 + tm * tn * 4
    return pl.pallas_call(
        _fox_out_kernel,
        out_shape=jax.ShapeDtypeStruct((n, d), F32),
        grid=(d // tn, n // tm),
        in_specs=[
            pl.BlockSpec((tm, d), lambda j, i: (i, 0)),
            pl.BlockSpec((None, d, tn), lambda j, i: (mixer, 0, j)),
            pl.BlockSpec((tm, tn), lambda j, i: (i, j)),
        ],
        out_specs=pl.BlockSpec((tm, tn), lambda j, i: (i, j)),
        compiler_params=pltpu.CompilerParams(
            dimension_semantics=("arbitrary", "arbitrary"),
            vmem_limit_bytes=_vmem_limit(blocks, temps)),
        name="fox_out",
    )(o, w_out, x)


def _fox(x, gains, layer, w_in_t, w_f_t, b_f, q_gains, k_gains, w_out, mixer, *, batch, seq, tm, tq):
    heads = x.shape[1] // FOX_HEAD_DIM
    qkv, f_terms, first = _fox_proj(x, gains, layer, w_in_t, w_f_t, b_f, q_gains, k_gains, mixer,
                                    seq=seq, tm=tm, tk=tq)
    first = first[:, :tm // tq, :heads].reshape(batch, seq // tq, heads).transpose(0, 2, 1).reshape(-1)
    o = _fox_attn(first, qkv, f_terms, batch=batch, seq=seq, heads=heads, tq=tq)
    return _fox_out(o, w_out, mixer, x, tm=tm, tn=min(1024, x.shape[1]))


def kernel(x, ffn1_norm, ffn1_w_gate, ffn1_w_up, ffn1_w_down, mix_norm, pool_w, pool_scale, fox_w_in, fox_b_f,
           fox_q_gain, fox_k_gain, fox_w_out, ffn2_norm, ffn2_w_gate, ffn2_w_up, ffn2_w_down):
    batch, seq, d = x.shape
    depth = ffn1_norm.shape[0]
    n_mixers = 2
    heads = d // FOX_HEAD_DIM
    tm = min(1024, seq)
    tq = min(512, seq)
    assert heads <= V7X_LANES

    w_in_t = jnp.swapaxes(fox_w_in, 1, 2)
    w_f_t = jnp.pad(w_in_t[:, 3 * d:, :], ((0, 0), (0, V7X_LANES - heads), (0, 0))).astype(BF16)
    b_f = jnp.pad(fox_b_f, ((0, 0), (0, V7X_LANES - heads))).reshape(-1, 1, V7X_LANES)

    h = x.reshape(batch * seq, d)
    for i in range(depth):
        h = _ffn(h, ffn1_norm, ffn1_w_gate, ffn1_w_up, ffn1_w_down, i, tm=tm)
        j = i // n_mixers
        if i % n_mixers == 0:
            h = _pool(h, mix_norm, i, pool_w, pool_scale, j, seq=seq, tm=min(512, seq))
        else:
            h = _fox(h, mix_norm, i, w_in_t, w_f_t, b_f, fox_q_gain, fox_k_gain, fox_w_out, j,
                     batch=batch, seq=seq, tm=tm, tq=tq)
        h = _ffn(h, ffn2_norm, ffn2_w_gate, ffn2_w_up, ffn2_w_down, i, tm=tm)
    return h.reshape(batch, seq, d)
```

```python
import functools
import math

import jax
import jax.numpy as jnp
from jax import lax
from jax.experimental import pallas as pl
from jax.experimental.pallas import tpu as pltpu

RMS_EPS = 1e-6
FFN_RESIDUAL_WEIGHT = 0.5
POOL_WINDOWS = (2, 4, 8, 16)
FOX_HEAD_DIM = 128
NEG_LARGE = -1e30
LOG2E = math.log2(math.e)

SKIP_LOG2_MARGIN = 160.0
QK_BOUND_SLACK = 1.01

V7X_LANES = 128
V7X_SUBLANES = 8
V7X_VMEM_BYTES = 64 * 1024 * 1024
POOL_HALO = 32
POOL_FIRST_ROW = 8
F_TERMS = 3
FFN_WEIGHT_SLOTS = 3

BF16 = jnp.bfloat16
F32 = jnp.float32


def _vmem_limit(block_bytes, temp_bytes):
    return min(2 * block_bytes + temp_bytes + (4 << 20), V7X_VMEM_BYTES - (2 << 20))


def _rms_norm(x, gain):
    ms = jnp.mean(x * x, axis=-1, keepdims=True)
    return x * lax.rsqrt(ms + RMS_EPS) * gain


def _dot(a, b):
    return jnp.dot(a, b, preferred_element_type=F32)


def _dot_nt(a, b):
    return lax.dot_general(a, b, (((1,), (1,)), ((), ())), preferred_element_type=F32)


def _split_bf16(x):
    terms = []
    for _ in range(F_TERMS):
        t = x.astype(BF16)
        terms.append(t)
        x = x - t.astype(F32)
    return terms


def _ffn_kernel(x_ref, g_ref, wg_hbm, wu_hbm, wd_hbm, o_ref, h_ref, wg_buf, wu_buf, wd_buf, sems,
                *, layer, first_step_chunk):
    i = pl.program_id(0)
    j = pl.program_id(1)
    n_j = pl.num_programs(1)
    last = n_j - 1
    tm = x_ref.shape[0]
    slots, _, tf = wg_buf.shape
    step = i * n_j + j
    n_steps = pl.num_programs(0) * n_j

    def tile_copies(s):
        col = pl.multiple_of(lax.rem(s, n_j) * tf, tf)
        slot = lax.rem(s, slots)
        return (pltpu.make_async_copy(wg_hbm.at[layer, :, pl.ds(col, tf)], wg_buf.at[slot], sems.at[0, slot]),
                pltpu.make_async_copy(wu_hbm.at[layer, :, pl.ds(col, tf)], wu_buf.at[slot], sems.at[1, slot]),
                pltpu.make_async_copy(wd_hbm.at[layer, pl.ds(col, tf), :], wd_buf.at[slot], sems.at[2, slot]))

    @pl.when(step == 0)
    def _():
        for s in range(slots - 1):
            for copy in tile_copies(jnp.int32(s)):
                copy.start()

    @pl.when(step + (slots - 1) < n_steps)
    def _():
        for copy in tile_copies(step + (slots - 1)):
            copy.start()

    for copy in tile_copies(step):
        copy.wait()
    slot = lax.rem(step, slots)

    def weights():
        return wg_buf[slot].astype(BF16), wu_buf[slot].astype(BF16), wd_buf[slot].astype(BF16)

    def partial_out(h, wg, wu, wd):
        gate = _dot(h, wg)
        up = _dot(h, wu)
        return _dot((gate * jax.nn.sigmoid(gate) * up).astype(BF16), wd)

    @pl.when(j == 0)
    def _():
        w = weights()
        for r0 in range(0, tm, first_step_chunk):
            rows = slice(r0, r0 + first_step_chunk)
            h = _rms_norm(x_ref[rows, :], g_ref[...]).astype(BF16)
            h_ref[rows, :] = h
            o_ref[rows, :] = partial_out(h, *w)

    @pl.when((j > 0) & (j < last))
    def _():
        o_ref[...] += partial_out(h_ref[...], *weights())

    @pl.when(j == last)
    def _():
        o_ref[...] = x_ref[...] + FFN_RESIDUAL_WEIGHT * (o_ref[...] + partial_out(h_ref[...], *weights()))


def _ffn(x, gains, w_gate, w_up, w_down, layer, *, tm, tf=256):
    n, d = x.shape
    f = w_gate.shape[2]
    assert n % tm == 0 and f % tf == 0 and f // tf >= 2
    slots = FFN_WEIGHT_SLOTS
    assert (n // tm) * (f // tf) >= slots - 1
    blocks = 2 * tm * d * 4 + d * 4
    temps = tm * d * 2 + slots * 3 * d * tf * 4 + 3 * d * tf * 2 + 3 * tm * tf * 4
    return pl.pallas_call(
        functools.partial(_ffn_kernel, layer=layer, first_step_chunk=min(256, tm)),
        out_shape=jax.ShapeDtypeStruct((n, d), F32),
        grid=(n // tm, f // tf),
        in_specs=[
            pl.BlockSpec((tm, d), lambda i, j: (i, 0)),
            pl.BlockSpec((None, 1, d), lambda i, j: (layer, 0, 0)),
            pl.BlockSpec(memory_space=pl.ANY),
            pl.BlockSpec(memory_space=pl.ANY),
            pl.BlockSpec(memory_space=pl.ANY),
        ],
        out_specs=pl.BlockSpec((tm, d), lambda i, j: (i, 0)),
        scratch_shapes=[pltpu.VMEM((tm, d), BF16),
                        pltpu.VMEM((slots, d, tf), F32), pltpu.VMEM((slots, d, tf), F32),
                        pltpu.VMEM((slots, tf, d), F32), pltpu.SemaphoreType.DMA((3, slots))],
        compiler_params=pltpu.CompilerParams(
            dimension_semantics=("arbitrary", "arbitrary"),
            vmem_limit_bytes=_vmem_limit(blocks, temps)),
        name="ffn",
    )(x, gains.reshape(gains.shape[0], 1, d), w_gate, w_up, w_down)


def _pool_kernel(x_ref, xh_ref, g_ref, w_ref, s_ref, o_ref, h_ref, p_ref, q_ref, *, tiles_per_batch):
    tm, d = x_ref.shape
    group = d // len(POOL_WINDOWS)
    rows = tm + POOL_HALO
    tile_in_batch = lax.rem(pl.program_id(0), tiles_per_batch)

    gain = g_ref[...]
    halo = _rms_norm(xh_ref[...], gain)
    h_ref[0:POOL_HALO, :] = jnp.where(tile_in_batch == 0, 0.0, halo)
    h_ref[POOL_HALO:, :] = _rms_norm(x_ref[...], gain)

    src = h_ref
    first_exact_row = POOL_FIRST_ROW
    for step in range(len(POOL_WINDOWS)):
        shift = 1 << step
        assert POOL_WINDOWS[step] == 2 * shift and shift <= POOL_FIRST_ROW
        first_exact_row += shift if step else 0
        dst = p_ref if step % 2 == 0 else q_ref
        c0 = step * group
        dst[POOL_FIRST_ROW:, c0:] = (src[POOL_FIRST_ROW:, c0:]
                                     + src[POOL_FIRST_ROW - shift:rows - shift, c0:])
        src = dst
    assert first_exact_row <= POOL_HALO

    pos = tile_in_batch * tm + lax.broadcasted_iota(jnp.int32, (tm, 1), 0)
    for gi, window in enumerate(POOL_WINDOWS):
        cols = slice(gi * group, (gi + 1) * group)
        sums = (p_ref if gi % 2 == 0 else q_ref)[POOL_HALO:, cols]
        count = jnp.minimum(pos + 1, window).astype(F32)
        diff = (sums / count - h_ref[POOL_HALO:, cols]).astype(BF16)
        y = _dot(diff, w_ref[gi].astype(BF16))
        o_ref[:, cols] = x_ref[:, cols] + y * s_ref[:, cols]


def _pool(x, gains, layer, w_groups, scales, mixer, *, seq, tm=512):
    n, d = x.shape
    _, groups, gsz, _ = w_groups.shape
    assert seq % tm == 0 and tm % POOL_HALO == 0 and groups * gsz == d
    halo_blocks = tm // POOL_HALO
    blocks = 2 * tm * d * 4 + POOL_HALO * d * 4 + groups * gsz * gsz * 4 + 2 * d * 4
    temps = 3 * (tm + POOL_HALO) * d * 4 + 2 * tm * d * 4
    return pl.pallas_call(
        functools.partial(_pool_kernel, tiles_per_batch=seq // tm),
        out_shape=jax.ShapeDtypeStruct((n, d), F32),
        grid=(n // tm,),
        in_specs=[
            pl.BlockSpec((tm, d), lambda i: (i, 0)),
            pl.BlockSpec((POOL_HALO, d), lambda i: (jnp.maximum(i * halo_blocks - 1, 0), 0)),
            pl.BlockSpec((None, 1, d), lambda i: (layer, 0, 0)),
            pl.BlockSpec((None, groups, gsz, gsz), lambda i: (mixer, 0, 0, 0)),
            pl.BlockSpec((None, 1, d), lambda i: (mixer, 0, 0)),
        ],
        out_specs=pl.BlockSpec((tm, d), lambda i: (i, 0)),
        scratch_shapes=[pltpu.VMEM((tm + POOL_HALO, d), F32)] * 3,
        compiler_params=pltpu.CompilerParams(
            dimension_semantics=("arbitrary",),
            vmem_limit_bytes=_vmem_limit(blocks, temps)),
        name="pool",
    )(x, x, gains.reshape(gains.shape[0], 1, d), w_groups, scales.reshape(scales.shape[0], 1, d))


def _log_sigmoid(z):
    return jnp.minimum(z, 0.0) - jnp.log1p(jnp.exp(-jnp.abs(z)))


def _fox_proj_kernel(x_ref, g_ref, w_ref, wf_ref, bf_ref, gq_ref, gk_ref, qkv_ref, f_ref, ks_ref,
                     h_ref, carry_ref, fmin_ref, *, tiles_per_batch, qk_tiles, tk, chunk):
    i = pl.program_id(0)
    j = pl.program_id(1)
    tm, tn = qkv_ref.shape
    blocks_per_tile = tm // tk

    @pl.when(j == 0)
    def _():
        h_ref[...] = _rms_norm(x_ref[...], g_ref[...]).astype(BF16)
        tile_in_batch = lax.rem(i, tiles_per_batch)

        @pl.when(tile_in_batch == 0)
        def _():
            carry_ref[...] = jnp.zeros_like(carry_ref)
            fmin_ref[...] = jnp.zeros_like(fmin_ref)

        r = lax.broadcasted_iota(jnp.int32, (chunk, chunk), 0)
        c = lax.broadcasted_iota(jnp.int32, (chunk, chunk), 1)
        tri = (c <= r).astype(BF16)
        carry = carry_ref[...]
        block_max = [None] * blocks_per_tile
        block_min = [None] * blocks_per_tile
        for c0 in range(0, tm, chunk):
            logit = _dot_nt(h_ref[c0:c0 + chunk, :], wf_ref[...]) + bf_ref[...]
            prefix = sum(_dot(tri, t) for t in _split_bf16(_log_sigmoid(logit)))
            total = prefix + carry
            carry = total[chunk - 1:chunk, :]
            f2 = total * LOG2E
            for t, term in enumerate(_split_bf16(f2)):
                f_ref[c0:c0 + chunk, t * V7X_LANES:(t + 1) * V7X_LANES] = term
            blk = c0 // tk
            hi = jnp.max(f2, axis=0, keepdims=True)
            lo = jnp.min(f2, axis=0, keepdims=True)
            block_max[blk] = hi if block_max[blk] is None else jnp.maximum(block_max[blk], hi)
            block_min[blk] = lo if block_min[blk] is None else jnp.minimum(block_min[blk], lo)
        carry_ref[...] = carry

        qk_bound = (LOG2E * math.sqrt(FOX_HEAD_DIM) * QK_BOUND_SLACK
                    * jnp.max(jnp.abs(gq_ref[...]), axis=-1, keepdims=True)
                    * jnp.max(jnp.abs(gk_ref[...]), axis=-1, keepdims=True))
        n_blocks = fmin_ref.shape[0]
        row = lax.broadcasted_iota(jnp.int32, fmin_ref.shape, 0)
        ks_ref[...] = jnp.zeros_like(ks_ref)
        for blk in range(blocks_per_tile):
            qb = tile_in_batch * blocks_per_tile + blk
            fmin_ref[pl.ds(qb, 1), :] = block_min[blk]
            gap = 2.0 * qk_bound + block_max[blk] - fmin_ref[...]
            needed = (gap > -SKIP_LOG2_MARGIN) | (row >= qb)
            ks_ref[blk:blk + 1, :] = jnp.min(jnp.where(needed, row, n_blocks), axis=0, keepdims=True)

    acc = _dot_nt(h_ref[...], w_ref[...].astype(BF16))

    @pl.when(j < qk_tiles)
    def _():
        is_q = j < qk_tiles // 2
        gain = jnp.where(is_q, gq_ref[...], gk_ref[...])
        post = jnp.where(is_q, LOG2E / math.sqrt(FOX_HEAD_DIM), 1.0)
        for c0 in range(0, tn, FOX_HEAD_DIM):
            y = _rms_norm(acc[:, c0:c0 + FOX_HEAD_DIM], gain) * post
            qkv_ref[:, c0:c0 + FOX_HEAD_DIM] = y.astype(BF16)

    @pl.when(j >= qk_tiles)
    def _():
        qkv_ref[...] = acc.astype(BF16)


def _fox_proj(x, gains, layer, w_in_t, w_f_t, b_f, q_gains, k_gains, mixer, *, seq, tm, tk, tn=512, chunk=256):
    n, d = x.shape
    assert seq % tm == 0 and tm % tk == 0 and tk % chunk == 0 and d % tn == 0 and tn % FOX_HEAD_DIM == 0
    assert tm // tk <= V7X_SUBLANES
    qk_tiles = 2 * d // tn
    f_cols = F_TERMS * V7X_LANES
    blocks = tm * d * 4 + d * tn * 4 + d * V7X_LANES * 2 + tm * tn * 2 + tm * f_cols * 2
    temps = tm * d * 2 + d * tn * 2 + 3 * tm * tn * 4 + 4 * chunk * chunk * 4
    return pl.pallas_call(
        functools.partial(_fox_proj_kernel, tiles_per_batch=seq // tm, qk_tiles=qk_tiles, tk=tk, chunk=chunk),
        out_shape=(jax.ShapeDtypeStruct((n, 3 * d), BF16),
                   jax.ShapeDtypeStruct((n, f_cols), BF16),
                   jax.ShapeDtypeStruct((n // tm, V7X_SUBLANES, V7X_LANES), jnp.int32)),
        grid=(n // tm, 3 * d // tn),
        in_specs=[
            pl.BlockSpec((tm, d), lambda i, j: (i, 0)),
            pl.BlockSpec((None, 1, d), lambda i, j: (layer, 0, 0)),
            pl.BlockSpec((None, tn, d), lambda i, j: (mixer, j, 0)),
            pl.BlockSpec((None, V7X_LANES, d), lambda i, j: (mixer, 0, 0)),
            pl.BlockSpec((None, 1, V7X_LANES), lambda i, j: (mixer, 0, 0)),
            pl.BlockSpec((None, 1, FOX_HEAD_DIM), lambda i, j: (mixer, 0, 0)),
            pl.BlockSpec((None, 1, FOX_HEAD_DIM), lambda i, j: (mixer, 0, 0)),
        ],
        out_specs=(pl.BlockSpec((tm, tn), lambda i, j: (i, j)),
                   pl.BlockSpec((tm, f_cols), lambda i, j: (i, 0)),
                   pl.BlockSpec((None, V7X_SUBLANES, V7X_LANES), lambda i, j: (i, 0, 0))),
        scratch_shapes=[pltpu.VMEM((tm, d), BF16), pltpu.VMEM((1, V7X_LANES), F32),
                        pltpu.VMEM((seq // tk, V7X_LANES), F32)],
        compiler_params=pltpu.CompilerParams(
            dimension_semantics=("arbitrary", "arbitrary"),
            vmem_limit_bytes=_vmem_limit(blocks, temps)),
        name="fox_proj",
    )(x, gains.reshape(gains.shape[0], 1, d), w_in_t, w_f_t, b_f,
      q_gains.reshape(q_gains.shape[0], 1, -1), k_gains.reshape(k_gains.shape[0], 1, -1))


def _fox_attn_kernel(first_ref, q_ref, k_ref, v_ref, f_ref, o_ref, kaug_ref, qaug_ref, s_ref, cmax_ref, m_ref,
                     l_ref, acc_ref, *, tq, tk):
    head = pl.program_id(1)
    seq = k_ref.shape[0]
    nq = seq // tq
    assert tq == tk
    table_base = (pl.program_id(0) * pl.num_programs(1) + head) * nq

    r = lax.broadcasted_iota(jnp.int32, (F_TERMS * V7X_LANES, V7X_LANES), 0)
    c = lax.broadcasted_iota(jnp.int32, (F_TERMS * V7X_LANES, V7X_LANES), 1)
    sel_q = (r == c * V7X_LANES + head).astype(BF16)
    sel_k = -(r == (c - F_TERMS) * V7X_LANES + head).astype(BF16)
    lane = lax.broadcasted_iota(jnp.int32, (1, V7X_LANES), 1)
    ones_q = ((lane >= F_TERMS) & (lane < 2 * F_TERMS)).astype(F32)
    ones_k = (lane < F_TERMS).astype(F32)

    kaug_ref[:, 0:FOX_HEAD_DIM] = k_ref[...]
    kaug_ref[:, FOX_HEAD_DIM:] = (_dot(f_ref[...], sel_k) + ones_k).astype(BF16)

    def rows_of(block, n_blocks):
        return pl.ds(pl.multiple_of(block * tk, tk), n_blocks * tk)

    def build_q(qi):
        qaug_ref[:, 0:FOX_HEAD_DIM] = q_ref[rows_of(qi, 1), :]
        qaug_ref[:, FOX_HEAD_DIM:] = (_dot(f_ref[rows_of(qi, 1), :], sel_q) + ones_q).astype(BF16)

    def logits(rows):
        return _dot_nt(kaug_ref[rows, :], qaug_ref[...])

    def tail_scores(qi, with_prev):
        s = logits(rows_of(qi - 1, 2) if with_prev else rows_of(qi, 1))
        key = lax.broadcasted_iota(jnp.int32, (tk, tq), 0)
        qry = lax.broadcasted_iota(jnp.int32, (tk, tq), 1)
        diag = jnp.where(key <= qry, s[-tk:], NEG_LARGE)
        cmax = jnp.max(diag, axis=0, keepdims=True)
        if with_prev:
            cmax = jnp.maximum(cmax, jnp.max(s[:tk], axis=0, keepdims=True))
            s_ref[0:tk, :] = s[:tk]
            s_ref[tk:, :] = diag
        else:
            s_ref[0:tk, :] = diag
        cmax_ref[...] = cmax

    def online_update(s, block_max, v_rows):
        m_old = m_ref[...]
        m_new = jnp.maximum(m_old, block_max)
        alpha = jnp.exp2(m_old - m_new)
        p = jnp.exp2(s - m_new)
        l_ref[...] = alpha * l_ref[...] + jnp.sum(p, axis=0, keepdims=True)
        pv = lax.dot_general(v_ref[v_rows, :], p.astype(BF16), (((0,), (0,)), ((), ())),
                             preferred_element_type=F32)
        acc_ref[...] = alpha * acc_ref[...] + pv
        m_ref[...] = m_new

    def tail_accumulate(qi, with_prev):
        if with_prev:
            online_update(s_ref[...], cmax_ref[...], rows_of(qi - 1, 2))
        else:
            online_update(s_ref[0:tk, :], cmax_ref[...], rows_of(qi, 1))

    def start_block(qi):
        m_ref[...] = jnp.full_like(m_ref, NEG_LARGE)
        l_ref[...] = jnp.zeros_like(l_ref)
        acc_ref[...] = jnp.zeros_like(acc_ref)

        def body(kb, c):
            s = logits(rows_of(kb, 1))
            online_update(s, jnp.max(s, axis=0, keepdims=True), rows_of(kb, 1))
            return c
        lax.fori_loop(first_ref[table_base + qi], qi - 1, body, 0)

    def finish_block(qi):
        o_ref[rows_of(qi, 1), :] = (acc_ref[...] / l_ref[...]).T.astype(BF16)

    build_q(0)
    tail_scores(0, False)
    start_block(0)
    build_q(1)
    tail_accumulate(0, False)
    tail_scores(1, True)
    finish_block(0)

    def query_block(qi, carry):
        start_block(qi)
        build_q(qi + 1)
        tail_accumulate(qi, True)
        tail_scores(qi + 1, True)
        finish_block(qi)
        return carry
    lax.fori_loop(1, nq - 1, query_block, 0)

    start_block(nq - 1)
    tail_accumulate(nq - 1, True)
    finish_block(nq - 1)


def _fox_attn(first_block, qkv, f_terms, *, batch, seq, heads, tq):
    n = qkv.shape[0]
    d = heads * FOX_HEAD_DIM
    nq = seq // tq
    f_cols = f_terms.shape[1]
    assert seq % tq == 0 and nq >= 2 and qkv.shape == (n, 3 * d) and first_block.shape == (batch * heads * nq,)
    blocks = 4 * seq * FOX_HEAD_DIM * 2 + seq * f_cols * 2
    temps = (seq + tq) * 2 * FOX_HEAD_DIM * 2 + tq * FOX_HEAD_DIM * 4 + 10 * tq * tq * 4
    return pl.pallas_call(
        functools.partial(_fox_attn_kernel, tq=tq, tk=tq),
        out_shape=jax.ShapeDtypeStruct((n, d), BF16),
        grid_spec=pltpu.PrefetchScalarGridSpec(
            num_scalar_prefetch=1,
            grid=(batch, heads),
            in_specs=[
                pl.BlockSpec((seq, FOX_HEAD_DIM), lambda b, h, first: (b, h)),
                pl.BlockSpec((seq, FOX_HEAD_DIM), lambda b, h, first: (b, heads + h)),
                pl.BlockSpec((seq, FOX_HEAD_DIM), lambda b, h, first: (b, 2 * heads + h)),
                pl.BlockSpec((seq, f_cols), lambda b, h, first: (b, 0)),
            ],
            out_specs=pl.BlockSpec((seq, FOX_HEAD_DIM), lambda b, h, first: (b, h)),
            scratch_shapes=[pltpu.VMEM((seq, 2 * FOX_HEAD_DIM), BF16), pltpu.VMEM((tq, 2 * FOX_HEAD_DIM), BF16),
                            pltpu.VMEM((2 * tq, tq), F32), pltpu.VMEM((1, tq), F32), pltpu.VMEM((1, tq), F32),
                            pltpu.VMEM((1, tq), F32), pltpu.VMEM((FOX_HEAD_DIM, tq), F32)]),
        compiler_params=pltpu.CompilerParams(
            dimension_semantics=("arbitrary", "arbitrary"),
            vmem_limit_bytes=_vmem_limit(blocks, temps)),
        name="fox_attn",
    )(first_block, qkv, qkv, qkv, f_terms)


def _fox_out_kernel(o_ref, w_ref, x_ref, y_ref):
    y_ref[...] = x_ref[...] + _dot(o_ref[...], w_ref[...].astype(BF16))


def _fox_out(o, w_out, mixer, x, *, tm, tn):
    n, d = x.shape
    assert n % tm == 0 and d % tn == 0
    blocks = tm * d * 2 + d * tn * 4 + 2 * tm * tn * 4
    temps = d * tn * 2 + tm * tn * 4
    return pl.pallas_call(
        _fox_out_kernel,
        out_shape=jax.ShapeDtypeStruct((n, d), F32),
        grid=(d // tn, n // tm),
        in_specs=[
            pl.BlockSpec((tm, d), lambda j, i: (i, 0)),
            pl.BlockSpec((None, d, tn), lambda j, i: (mixer, 0, j)),
            pl.BlockSpec((tm, tn), lambda j, i: (i, j)),
        ],
        out_specs=pl.BlockSpec((tm, tn), lambda j, i: (i, j)),
        compiler_params=pltpu.CompilerParams(
            dimension_semantics=("arbitrary", "arbitrary"),
            vmem_limit_bytes=_vmem_limit(blocks, temps)),
        name="fox_out",
    )(o, w_out, x)


def _fox(x, gains, layer, w_in_t, w_f_t, b_f, q_gains, k_gains, w_out, mixer, *, batch, seq, tm, tq):
    heads = x.shape[1] // FOX_HEAD_DIM
    qkv, f_terms, first = _fox_proj(x, gains, layer, w_in_t, w_f_t, b_f, q_gains, k_gains, mixer,
                                    seq=seq, tm=tm, tk=tq)
    first = first[:, :tm // tq, :heads].reshape(batch, seq // tq, heads).transpose(0, 2, 1).reshape(-1)
    o = _fox_attn(first, qkv, f_terms, batch=batch, seq=seq, heads=heads, tq=tq)
    return _fox_out(o, w_out, mixer, x, tm=tm, tn=min(1024, x.shape[1]))


def kernel(x, ffn1_norm, ffn1_w_gate, ffn1_w_up, ffn1_w_down, mix_norm, pool_w, pool_scale, fox_w_in, fox_b_f,
           fox_q_gain, fox_k_gain, fox_w_out, ffn2_norm, ffn2_w_gate, ffn2_w_up, ffn2_w_down):
    batch, seq, d = x.shape
    depth = ffn1_norm.shape[0]
    n_mixers = 2
    heads = d // FOX_HEAD_DIM
    tm = min(1024, seq)
    tq = min(512, seq)
    assert heads <= V7X_LANES

    w_in_t = jnp.swapaxes(fox_w_in, 1, 2)
    w_f_t = jnp.pad(w_in_t[:, 3 * d:, :], ((0, 0), (0, V7X_LANES - heads), (0, 0))).astype(BF16)
    b_f = jnp.pad(fox_b_f, ((0, 0), (0, V7X_LANES - heads))).reshape(-1, 1, V7X_LANES)

    h = x.reshape(batch * seq, d)
    for i in range(depth):
        h = _ffn(h, ffn1_norm, ffn1_w_gate, ffn1_w_up, ffn1_w_down, i, tm=tm)
        j = i // n_mixers
        if i % n_mixers == 0:
            h = _pool(h, mix_norm, i, pool_w, pool_scale, j, seq=seq, tm=min(512, seq))
        else:
            h = _fox(h, mix_norm, i, w_in_t, w_f_t, b_f, fox_q_gain, fox_k_gain, fox_w_out, j,
                     batch=batch, seq=seq, tm=tm, tq=tq)
        h = _ffn(h, ffn2_norm, ffn2_w_gate, ffn2_w_up, ffn2_w_down, i, tm=tm)
    return h.reshape(batch, seq, d)
```

```python
import functools
import math

import jax
import jax.numpy as jnp
from jax import lax
from jax.experimental import pallas as pl
from jax.experimental.pallas import tpu as pltpu

RMS_EPS = 1e-6
FFN_RESIDUAL_WEIGHT = 0.5
POOL_WINDOWS = (2, 4, 8, 16)
FOX_HEAD_DIM = 128
NEG_LARGE = -1e30
LOG2E = math.log2(math.e)

SKIP_LOG2_MARGIN = 160.0
QK_BOUND_SLACK = 1.01

V7X_LANES = 128
V7X_SUBLANES = 8
V7X_VMEM_BYTES = 64 * 1024 * 1024
POOL_HALO = 32
POOL_FIRST_ROW = 8
F_TERMS = 3

BF16 = jnp.bfloat16
F32 = jnp.float32


def _vmem_limit(block_bytes, temp_bytes):
    return min(2 * block_bytes + temp_bytes + (4 << 20), V7X_VMEM_BYTES - (2 << 20))


def _rms_norm(x, gain):
    ms = jnp.mean(x * x, axis=-1, keepdims=True)
    return x * lax.rsqrt(ms + RMS_EPS) * gain


def _dot(a, b):
    return jnp.dot(a, b, preferred_element_type=F32)


def _dot_nt(a, b):
    return lax.dot_general(a, b, (((1,), (1,)), ((), ())), preferred_element_type=F32)


def _split_bf16(x):
    terms = []
    for _ in range(F_TERMS):
        t = x.astype(BF16)
        terms.append(t)
        x = x - t.astype(F32)
    return terms


def _ffn_kernel(x_hbm, g_ref, wg_hbm, wu_hbm, wd_hbm, o_ref, x_buf, h_ref, wg_buf, wu_buf, wd_buf, w_sems, x_sem,
                *, layer, first_step_chunk):
    i = pl.program_id(0)
    j = pl.program_id(1)
    n_i = pl.num_programs(0)
    n_j = pl.num_programs(1)
    last = n_j - 1
    tm = x_buf.shape[0]
    slots, _, tf = wg_buf.shape
    per_step = slots // 2
    step = i * n_j + j
    n_steps = n_i * n_j

    def tile_copies(t):
        col = pl.multiple_of(lax.rem(t, n_j * per_step) * tf, tf)
        slot = lax.rem(t, slots)
        return (pltpu.make_async_copy(wg_hbm.at[layer, :, pl.ds(col, tf)], wg_buf.at[slot], w_sems.at[0, slot]),
                pltpu.make_async_copy(wu_hbm.at[layer, :, pl.ds(col, tf)], wu_buf.at[slot], w_sems.at[1, slot]),
                pltpu.make_async_copy(wd_hbm.at[layer, pl.ds(col, tf), :], wd_buf.at[slot], w_sems.at[2, slot]))

    def x_copy(tile):
        return pltpu.make_async_copy(x_hbm.at[pl.ds(pl.multiple_of(tile * tm, tm), tm), :], x_buf, x_sem)

    @pl.when(step == 0)
    def _():
        x_copy(0).start()
        for t in range(per_step):
            for copy in tile_copies(jnp.int32(t)):
                copy.start()

    @pl.when(step + 1 < n_steps)
    def _():
        for k in range(per_step):
            for copy in tile_copies((step + 1) * per_step + k):
                copy.start()

    @pl.when((j == 1) & (i + 1 < n_i))
    def _():
        x_copy(i + 1).start()

    @pl.when(j == 0)
    def _():
        x_copy(i).wait()

    for k in range(per_step):
        for copy in tile_copies(step * per_step + k):
            copy.wait()

    def weights():
        tiles = []
        for k in range(per_step):
            slot = lax.rem(step * per_step + k, slots)
            tiles.append((wg_buf[slot].astype(BF16), wu_buf[slot].astype(BF16), wd_buf[slot].astype(BF16)))
        return tiles

    def partial_out(h, tiles):
        out = None
        for wg, wu, wd in tiles:
            gate = _dot(h, wg)
            up = _dot(h, wu)
            term = _dot((gate * jax.nn.sigmoid(gate) * up).astype(BF16), wd)
            out = term if out is None else out + term
        return out

    @pl.when(j == 0)
    def _():
        tiles = weights()
        for r0 in range(0, tm, first_step_chunk):
            rows = slice(r0, r0 + first_step_chunk)
            x = x_buf[rows, :]
            h = _rms_norm(x, g_ref[...]).astype(BF16)
            h_ref[rows, :] = h
            o_ref[rows, :] = x * (1.0 / FFN_RESIDUAL_WEIGHT) + partial_out(h, tiles)

    @pl.when((j > 0) & (j < last))
    def _():
        o_ref[...] += partial_out(h_ref[...], weights())

    @pl.when(j == last)
    def _():
        o_ref[...] = FFN_RESIDUAL_WEIGHT * (o_ref[...] + partial_out(h_ref[...], weights()))


def _ffn(x, gains, w_gate, w_up, w_down, layer, *, tm, tf=256, tiles_per_step=2):
    n, d = x.shape
    f = w_gate.shape[2]
    n_j = f // (tf * tiles_per_step)
    slots = 2 * tiles_per_step
    assert n % tm == 0 and f % (tf * tiles_per_step) == 0 and n_j >= 2
    assert math.frexp(FFN_RESIDUAL_WEIGHT)[0] == 0.5, "residual weight must be a power of two"
    blocks = tm * d * 4 + d * 4
    temps = (tm * d * 4 + tm * d * 2 + slots * 3 * d * tf * 4
             + tiles_per_step * (3 * d * tf * 2 + 3 * tm * tf * 4))
    return pl.pallas_call(
        functools.partial(_ffn_kernel, layer=layer, first_step_chunk=min(256, tm)),
        out_shape=jax.ShapeDtypeStruct((n, d), F32),
        grid=(n // tm, n_j),
        in_specs=[
            pl.BlockSpec(memory_space=pl.ANY),
            pl.BlockSpec((None, 1, d), lambda i, j: (layer, 0, 0)),
            pl.BlockSpec(memory_space=pl.ANY),
            pl.BlockSpec(memory_space=pl.ANY),
            pl.BlockSpec(memory_space=pl.ANY),
        ],
        out_specs=pl.BlockSpec((tm, d), lambda i, j: (i, 0)),
        scratch_shapes=[pltpu.VMEM((tm, d), F32), pltpu.VMEM((tm, d), BF16),
                        pltpu.VMEM((slots, d, tf), F32), pltpu.VMEM((slots, d, tf), F32),
                        pltpu.VMEM((slots, tf, d), F32), pltpu.SemaphoreType.DMA((3, slots)),
                        pltpu.SemaphoreType.DMA(())],
        compiler_params=pltpu.CompilerParams(
            dimension_semantics=("arbitrary", "arbitrary"),
            vmem_limit_bytes=_vmem_limit(blocks, temps)),
        name="ffn",
    )(x, gains.reshape(gains.shape[0], 1, d), w_gate, w_up, w_down)


def _pool_kernel(x_ref, xh_ref, g_ref, w_ref, s_ref, o_ref, h_ref, p_ref, q_ref, *, tiles_per_batch):
    tm, d = x_ref.shape
    group = d // len(POOL_WINDOWS)
    rows = tm + POOL_HALO
    tile_in_batch = lax.rem(pl.program_id(0), tiles_per_batch)

    gain = g_ref[...]
    halo = _rms_norm(xh_ref[...], gain)
    h_ref[0:POOL_HALO, :] = jnp.where(tile_in_batch == 0, 0.0, halo)
    h_ref[POOL_HALO:, :] = _rms_norm(x_ref[...], gain)

    src = h_ref
    first_exact_row = POOL_FIRST_ROW
    for step in range(len(POOL_WINDOWS)):
        shift = 1 << step
        assert POOL_WINDOWS[step] == 2 * shift and shift <= POOL_FIRST_ROW
        first_exact_row += shift if step else 0
        dst = p_ref if step % 2 == 0 else q_ref
        c0 = step * group
        dst[POOL_FIRST_ROW:, c0:] = (src[POOL_FIRST_ROW:, c0:]
                                     + src[POOL_FIRST_ROW - shift:rows - shift, c0:])
        src = dst
    assert first_exact_row <= POOL_HALO

    pos = tile_in_batch * tm + lax.broadcasted_iota(jnp.int32, (tm, 1), 0)
    for gi, window in enumerate(POOL_WINDOWS):
        cols = slice(gi * group, (gi + 1) * group)
        sums = (p_ref if gi % 2 == 0 else q_ref)[POOL_HALO:, cols]
        count = jnp.minimum(pos + 1, window).astype(F32)
        diff = (sums / count - h_ref[POOL_HALO:, cols]).astype(BF16)
        y = _dot(diff, w_ref[gi].astype(BF16))
        o_ref[:, cols] = x_ref[:, cols] + y * s_ref[:, cols]


def _pool(x, gains, layer, w_groups, scales, mixer, *, seq, tm=512):
    n, d = x.shape
    _, groups, gsz, _ = w_groups.shape
    assert seq % tm == 0 and tm % POOL_HALO == 0 and groups * gsz == d
    halo_blocks = tm // POOL_HALO
    blocks = 2 * tm * d * 4 + POOL_HALO * d * 4 + groups * gsz * gsz * 4 + 2 * d * 4
    temps = 3 * (tm + POOL_HALO) * d * 4 + 2 * tm * d * 4
    return pl.pallas_call(
        functools.partial(_pool_kernel, tiles_per_batch=seq // tm),
        out_shape=jax.ShapeDtypeStruct((n, d), F32),
        grid=(n // tm,),
        in_specs=[
            pl.BlockSpec((tm, d), lambda i: (i, 0)),
            pl.BlockSpec((POOL_HALO, d), lambda i: (jnp.maximum(i * halo_blocks - 1, 0), 0)),
            pl.BlockSpec((None, 1, d), lambda i: (layer, 0, 0)),
            pl.BlockSpec((None, groups, gsz, gsz), lambda i: (mixer, 0, 0, 0)),
            pl.BlockSpec((None, 1, d), lambda i: (mixer, 0, 0)),
        ],
        out_specs=pl.BlockSpec((tm, d), lambda i: (i, 0)),
        scratch_shapes=[pltpu.VMEM((tm + POOL_HALO, d), F32)] * 3,
        compiler_params=pltpu.CompilerParams(
            dimension_semantics=("arbitrary",),
            vmem_limit_bytes=_vmem_limit(blocks, temps)),
        name="pool",
    )(x, x, gains.reshape(gains.shape[0], 1, d), w_groups, scales.reshape(scales.shape[0], 1, d))


def _log_sigmoid(z):
    return jnp.minimum(z, 0.0) - jnp.log1p(jnp.exp(-jnp.abs(z)))


def _fox_proj_kernel(x_ref, g_ref, w_ref, wf_ref, bf_ref, gq_ref, gk_ref, qkv_ref, f_ref, ks_ref,
                     h_ref, carry_ref, fmin_ref, *, tiles_per_batch, qk_tiles, tk, chunk):
    i = pl.program_id(0)
    j = pl.program_id(1)
    tm, tn = qkv_ref.shape
    blocks_per_tile = tm // tk

    @pl.when(j == 0)
    def _():
        h_ref[...] = _rms_norm(x_ref[...], g_ref[...]).astype(BF16)
        tile_in_batch = lax.rem(i, tiles_per_batch)

        @pl.when(tile_in_batch == 0)
        def _():
            carry_ref[...] = jnp.zeros_like(carry_ref)
            fmin_ref[...] = jnp.zeros_like(fmin_ref)

        r = lax.broadcasted_iota(jnp.int32, (chunk, chunk), 0)
        c = lax.broadcasted_iota(jnp.int32, (chunk, chunk), 1)
        tri = (c <= r).astype(BF16)
        carry = carry_ref[...]
        block_max = [None] * blocks_per_tile
        block_min = [None] * blocks_per_tile
        for c0 in range(0, tm, chunk):
            logit = _dot_nt(h_ref[c0:c0 + chunk, :], wf_ref[...]) + bf_ref[...]
            prefix = sum(_dot(tri, t) for t in _split_bf16(_log_sigmoid(logit)))
            total = prefix + carry
            carry = total[chunk - 1:chunk, :]
            f2 = total * LOG2E
            for t, term in enumerate(_split_bf16(f2)):
                f_ref[c0:c0 + chunk, t * V7X_LANES:(t + 1) * V7X_LANES] = term
            blk = c0 // tk
            hi = jnp.max(f2, axis=0, keepdims=True)
            lo = jnp.min(f2, axis=0, keepdims=True)
            block_max[blk] = hi if block_max[blk] is None else jnp.maximum(block_max[blk], hi)
            block_min[blk] = lo if block_min[blk] is None else jnp.minimum(block_min[blk], lo)
        carry_ref[...] = carry

        qk_bound = (LOG2E * math.sqrt(FOX_HEAD_DIM) * QK_BOUND_SLACK
                    * jnp.max(jnp.abs(gq_ref[...]), axis=-1, keepdims=True)
                    * jnp.max(jnp.abs(gk_ref[...]), axis=-1, keepdims=True))
        n_blocks = fmin_ref.shape[0]
        row = lax.broadcasted_iota(jnp.int32, fmin_ref.shape, 0)
        ks_ref[...] = jnp.zeros_like(ks_ref)
        for blk in range(blocks_per_tile):
            qb = tile_in_batch * blocks_per_tile + blk
            fmin_ref[pl.ds(qb, 1), :] = block_min[blk]
            gap = 2.0 * qk_bound + block_max[blk] - fmin_ref[...]
            needed = (gap > -SKIP_LOG2_MARGIN) | (row >= qb)
            ks_ref[blk:blk + 1, :] = jnp.min(jnp.where(needed, row, n_blocks), axis=0, keepdims=True)

    acc = _dot_nt(h_ref[...], w_ref[...].astype(BF16))

    @pl.when(j < qk_tiles)
    def _():
        is_q = j < qk_tiles // 2
        gain = jnp.where(is_q, gq_ref[...], gk_ref[...])
        post = jnp.where(is_q, LOG2E / math.sqrt(FOX_HEAD_DIM), 1.0)
        for c0 in range(0, tn, FOX_HEAD_DIM):
            y = _rms_norm(acc[:, c0:c0 + FOX_HEAD_DIM], gain) * post
            qkv_ref[:, c0:c0 + FOX_HEAD_DIM] = y.astype(BF16)

    @pl.when(j >= qk_tiles)
    def _():
        qkv_ref[...] = acc.astype(BF16)


def _fox_proj(x, gains, layer, w_in_t, w_f_t, b_f, q_gains, k_gains, mixer, *, seq, tm, tk, tn=512, chunk=256):
    n, d = x.shape
    assert seq % tm == 0 and tm % tk == 0 and tk % chunk == 0 and d % tn == 0 and tn % FOX_HEAD_DIM == 0
    assert tm // tk <= V7X_SUBLANES
    qk_tiles = 2 * d // tn
    f_cols = F_TERMS * V7X_LANES
    blocks = tm * d * 4 + d * tn * 4 + d * V7X_LANES * 2 + tm * tn * 2 + tm * f_cols * 2
    temps = tm * d * 2 + d * tn * 2 + 3 * tm * tn * 4 + 4 * chunk * chunk * 4
    return pl.pallas_call(
        functools.partial(_fox_proj_kernel, tiles_per_batch=seq // tm, qk_tiles=qk_tiles, tk=tk, chunk=chunk),
        out_shape=(jax.ShapeDtypeStruct((n, 3 * d), BF16),
                   jax.ShapeDtypeStruct((n, f_cols), BF16),
                   jax.ShapeDtypeStruct((n // tm, V7X_SUBLANES, V7X_LANES), jnp.int32)),
        grid=(n // tm, 3 * d // tn),
        in_specs=[
            pl.BlockSpec((tm, d), lambda i, j: (i, 0)),
            pl.BlockSpec((None, 1, d), lambda i, j: (layer, 0, 0)),
            pl.BlockSpec((None, tn, d), lambda i, j: (mixer, j, 0)),
            pl.BlockSpec((None, V7X_LANES, d), lambda i, j: (mixer, 0, 0)),
            pl.BlockSpec((None, 1, V7X_LANES), lambda i, j: (mixer, 0, 0)),
            pl.BlockSpec((None, 1, FOX_HEAD_DIM), lambda i, j: (mixer, 0, 0)),
            pl.BlockSpec((None, 1, FOX_HEAD_DIM), lambda i, j: (mixer, 0, 0)),
        ],
        out_specs=(pl.BlockSpec((tm, tn), lambda i, j: (i, j)),
                   pl.BlockSpec((tm, f_cols), lambda i, j: (i, 0)),
                   pl.BlockSpec((None, V7X_SUBLANES, V7X_LANES), lambda i, j: (i, 0, 0))),
        scratch_shapes=[pltpu.VMEM((tm, d), BF16), pltpu.VMEM((1, V7X_LANES), F32),
                        pltpu.VMEM((seq // tk, V7X_LANES), F32)],
        compiler_params=pltpu.CompilerParams(
            dimension_semantics=("arbitrary", "arbitrary"),
            vmem_limit_bytes=_vmem_limit(blocks, temps)),
        name="fox_proj",
    )(x, gains.reshape(gains.shape[0], 1, d), w_in_t, w_f_t, b_f,
      q_gains.reshape(q_gains.shape[0], 1, -1), k_gains.reshape(k_gains.shape[0], 1, -1))


def _fox_attn_kernel(first_ref, q_ref, k_ref, v_ref, f_ref, o_ref, kaug_ref, qaug_ref, s_ref, cmax_ref, m_ref,
                     l_ref, acc_ref, *, tq, tk):
    head = pl.program_id(1)
    seq = k_ref.shape[0]
    nq = seq // tq
    assert tq == tk
    table_base = (pl.program_id(0) * pl.num_programs(1) + head) * nq

    r = lax.broadcasted_iota(jnp.int32, (F_TERMS * V7X_LANES, V7X_LANES), 0)
    c = lax.broadcasted_iota(jnp.int32, (F_TERMS * V7X_LANES, V7X_LANES), 1)
    sel_q = (r == c * V7X_LANES + head).astype(BF16)
    sel_k = -(r == (c - F_TERMS) * V7X_LANES + head).astype(BF16)
    lane = lax.broadcasted_iota(jnp.int32, (1, V7X_LANES), 1)
    ones_q = ((lane >= F_TERMS) & (lane < 2 * F_TERMS)).astype(F32)
    ones_k = (lane < F_TERMS).astype(F32)

    kaug_ref[:, 0:FOX_HEAD_DIM] = k_ref[...]
    kaug_ref[:, FOX_HEAD_DIM:] = (_dot(f_ref[...], sel_k) + ones_k).astype(BF16)

    def rows_of(block, n_blocks):
        return pl.ds(pl.multiple_of(block * tk, tk), n_blocks * tk)

    def build_q(qi):
        qaug_ref[:, 0:FOX_HEAD_DIM] = q_ref[rows_of(qi, 1), :]
        qaug_ref[:, FOX_HEAD_DIM:] = (_dot(f_ref[rows_of(qi, 1), :], sel_q) + ones_q).astype(BF16)

    def logits(rows):
        return _dot_nt(kaug_ref[rows, :], qaug_ref[...])

    def tail_scores(qi, with_prev):
        s = logits(rows_of(qi - 1, 2) if with_prev else rows_of(qi, 1))
        key = lax.broadcasted_iota(jnp.int32, (tk, tq), 0)
        qry = lax.broadcasted_iota(jnp.int32, (tk, tq), 1)
        diag = jnp.where(key <= qry, s[-tk:], NEG_LARGE)
        cmax = jnp.max(diag, axis=0, keepdims=True)
        if with_prev:
            cmax = jnp.maximum(cmax, jnp.max(s[:tk], axis=0, keepdims=True))
            s_ref[0:tk, :] = s[:tk]
            s_ref[tk:, :] = diag
        else:
            s_ref[0:tk, :] = diag
        cmax_ref[...] = cmax

    def online_update(s, block_max, v_rows):
        m_old = m_ref[...]
        m_new = jnp.maximum(m_old, block_max)
        alpha = jnp.exp2(m_old - m_new)
        p = jnp.exp2(s - m_new)
        l_ref[...] = alpha * l_ref[...] + jnp.sum(p, axis=0, keepdims=True)
        pv = lax.dot_general(v_ref[v_rows, :], p.astype(BF16), (((0,), (0,)), ((), ())),
                             preferred_element_type=F32)
        acc_ref[...] = alpha * acc_ref[...] + pv
        m_ref[...] = m_new

    def tail_accumulate(qi, with_prev):
        if with_prev:
            online_update(s_ref[...], cmax_ref[...], rows_of(qi - 1, 2))
        else:
            online_update(s_ref[0:tk, :], cmax_ref[...], rows_of(qi, 1))

    def start_block(qi):
        m_ref[...] = jnp.full_like(m_ref, NEG_LARGE)
        l_ref[...] = jnp.zeros_like(l_ref)
        acc_ref[...] = jnp.zeros_like(acc_ref)

        def body(kb, c):
            s = logits(rows_of(kb, 1))
            online_update(s, jnp.max(s, axis=0, keepdims=True), rows_of(kb, 1))
            return c
        lax.fori_loop(first_ref[table_base + qi], qi - 1, body, 0)

    def finish_block(qi):
        o_ref[rows_of(qi, 1), :] = (acc_ref[...] / l_ref[...]).T.astype(BF16)

    build_q(0)
    tail_scores(0, False)
    start_block(0)
    build_q(1)
    tail_accumulate(0, False)
    tail_scores(1, True)
    finish_block(0)

    def query_block(qi, carry):
        start_block(qi)
        build_q(qi + 1)
        tail_accumulate(qi, True)
        tail_scores(qi + 1, True)
        finish_block(qi)
        return carry
    lax.fori_loop(1, nq - 1, query_block, 0)

    start_block(nq - 1)
    tail_accumulate(nq - 1, True)
    finish_block(nq - 1)


def _fox_attn(first_block, qkv, f_terms, *, batch, seq, heads, tq):
    n = qkv.shape[0]
    d = heads * FOX_HEAD_DIM
    nq = seq // tq
    f_cols = f_terms.shape[1]
    assert seq % tq == 0 and nq >= 2 and qkv.shape == (n, 3 * d) and first_block.shape == (batch * heads * nq,)
    blocks = 4 * seq * FOX_HEAD_DIM * 2 + seq * f_cols * 2
    temps = (seq + tq) * 2 * FOX_HEAD_DIM * 2 + tq * FOX_HEAD_DIM * 4 + 10 * tq * tq * 4
    return pl.pallas_call(
        functools.partial(_fox_attn_kernel, tq=tq, tk=tq),
        out_shape=jax.ShapeDtypeStruct((n, d), BF16),
        grid_spec=pltpu.PrefetchScalarGridSpec(
            num_scalar_prefetch=1,
            grid=(batch, heads),
            in_specs=[
                pl.BlockSpec((seq, FOX_HEAD_DIM), lambda b, h, first: (b, h)),
                pl.BlockSpec((seq, FOX_HEAD_DIM), lambda b, h, first: (b, heads + h)),
                pl.BlockSpec((seq, FOX_HEAD_DIM), lambda b, h, first: (b, 2 * heads + h)),
                pl.BlockSpec((seq, f_cols), lambda b, h, first: (b, 0)),
            ],
            out_specs=pl.BlockSpec((seq, FOX_HEAD_DIM), lambda b, h, first: (b, h)),
            scratch_shapes=[pltpu.VMEM((seq, 2 * FOX_HEAD_DIM), BF16), pltpu.VMEM((tq, 2 * FOX_HEAD_DIM), BF16),
                            pltpu.VMEM((2 * tq, tq), F32), pltpu.VMEM((1, tq), F32), pltpu.VMEM((1, tq), F32),
                            pltpu.VMEM((1, tq), F32), pltpu.VMEM((FOX_HEAD_DIM, tq), F32)]),
        compiler_params=pltpu.CompilerParams(
            dimension_semantics=("arbitrary", "arbitrary"),
            vmem_limit_bytes=_vmem_limit(blocks, temps)),
        name="fox_attn",
    )(first_block, qkv, qkv, qkv, f_terms)


def _fox_out_kernel(o_ref, w_ref, x_ref, y_ref):
    y_ref[...] = x_ref[...] + _dot(o_ref[...], w_ref[...].astype(BF16))


def _fox_out(o, w_out, mixer, x, *, tm, tn):
    n, d = x.shape
    assert n % tm == 0 and d % tn == 0
    blocks = tm * d * 2 + d * tn * 4 + 2 * tm * tn * 4
    temps = d * tn * 2 + tm * tn * 4
    return pl.pallas_call(
        _fox_out_kernel,
        out_shape=jax.ShapeDtypeStruct((n, d), F32),
        grid=(d // tn, n // tm),
        in_specs=[
            pl.BlockSpec((tm, d), lambda j, i: (i, 0)),
            pl.BlockSpec((None, d, tn), lambda j, i: (mixer, 0, j)),
            pl.BlockSpec((tm, tn), lambda j, i: (i, j)),
        ],
        out_specs=pl.BlockSpec((tm, tn), lambda j, i: (i, j)),
        compiler_params=pltpu.CompilerParams(
            dimension_semantics=("arbitrary", "arbitrary"),
            vmem_limit_bytes=_vmem_limit(blocks, temps)),
        name="fox_out",
    )(o, w_out, x)


def _fox(x, gains, layer, w_in_t, w_f_t, b_f, q_gains, k_gains, w_out, mixer, *, batch, seq, tm, tq):
    heads = x.shape[1] // FOX_HEAD_DIM
    qkv, f_terms, first = _fox_proj(x, gains, layer, w_in_t, w_f_t, b_f, q_gains, k_gains, mixer,
                                    seq=seq, tm=tm, tk=tq)
    first = first[:, :tm // tq, :heads].reshape(batch, seq // tq, heads).transpose(0, 2, 1).reshape(-1)
    o = _fox_attn(first, qkv, f_terms, batch=batch, seq=seq, heads=heads, tq=tq)
    return _fox_out(o, w_out, mixer, x, tm=tm, tn=min(1024, x.shape[1]))


def kernel(x, ffn1_norm, ffn1_w_gate, ffn1_w_up, ffn1_w_down, mix_norm, pool_w, pool_scale, fox_w_in, fox_b_f,
           fox_q_gain, fox_k_gain, fox_w_out, ffn2_norm, ffn2_w_gate, ffn2_w_up, ffn2_w_down):
    batch, seq, d = x.shape
    depth = ffn1_norm.shape[0]
    n_mixers = 2
    heads = d // FOX_HEAD_DIM
    tm = min(1024, seq)
    tq = min(512, seq)
    assert heads <= V7X_LANES

    w_in_t = jnp.swapaxes(fox_w_in, 1, 2)
    w_f_t = jnp.pad(w_in_t[:, 3 * d:, :], ((0, 0), (0, V7X_LANES - heads), (0, 0))).astype(BF16)
    b_f = jnp.pad(fox_b_f, ((0, 0), (0, V7X_LANES - heads))).reshape(-1, 1, V7X_LANES)

    h = x.reshape(batch * seq, d)
    for i in range(depth):
        h = _ffn(h, ffn1_norm, ffn1_w_gate, ffn1_w_up, ffn1_w_down, i, tm=tm)
        j = i // n_mixers
        if i % n_mixers == 0:
            h = _pool(h, mix_norm, i, pool_w, pool_scale, j, seq=seq, tm=min(512, seq))
        else:
            h = _fox(h, mix_norm, i, w_in_t, w_f_t, b_f, fox_q_gain, fox_k_gain, fox_w_out, j,
                     batch=batch, seq=seq, tm=tm, tq=tq)
        h = _ffn(h, ffn2_norm, ffn2_w_gate, ffn2_w_up, ffn2_w_down, i, tm=tm)
    return h.reshape(batch, seq, d)
```

```python
import functools
import math

import jax
import jax.numpy as jnp
from jax import lax
from jax.experimental import pallas as pl
from jax.experimental.pallas import tpu as pltpu

RMS_EPS = 1e-6
FFN_RESIDUAL_WEIGHT = 0.5
POOL_WINDOWS = (2, 4, 8, 16)
FOX_HEAD_DIM = 128
NEG_LARGE = -1e30
LOG2E = math.log2(math.e)

SKIP_LOG2_MARGIN = 160.0
QK_BOUND_SLACK = 1.01

V7X_LANES = 128
V7X_SUBLANES = 8
V7X_VMEM_BYTES = 64 * 1024 * 1024
POOL_HALO = 32
POOL_FIRST_ROW = 8
F_TERMS = 3

BF16 = jnp.bfloat16
F32 = jnp.float32


def _vmem_limit(block_bytes, temp_bytes):
    return min(2 * block_bytes + temp_bytes + (4 << 20), V7X_VMEM_BYTES - (2 << 20))


def _rms_norm(x, gain):
    ms = jnp.mean(x * x, axis=-1, keepdims=True)
    return x * lax.rsqrt(ms + RMS_EPS) * gain


def _dot(a, b):
    return jnp.dot(a, b, preferred_element_type=F32)


def _dot_nt(a, b):
    return lax.dot_general(a, b, (((1,), (1,)), ((), ())), preferred_element_type=F32)


def _split_bf16(x):
    terms = []
    for _ in range(F_TERMS):
        t = x.astype(BF16)
        terms.append(t)
        x = x - t.astype(F32)
    return terms


def _ffn_kernel(x_hbm, g_ref, wg_hbm, wu_hbm, wd_hbm, o_ref, x_buf, h_ref, wg_buf, wu_buf, wd_buf, w_sems, x_sem,
                *, layer, first_step_chunk):
    i = pl.program_id(0)
    j = pl.program_id(1)
    n_i = pl.num_programs(0)
    n_j = pl.num_programs(1)
    last = n_j - 1
    tm = x_buf.shape[0]
    slots, _, tf = wg_buf.shape
    per_step = slots // 2
    step = i * n_j + j
    n_steps = n_i * n_j

    def tile_copies(t):
        col = pl.multiple_of(lax.rem(t, n_j * per_step) * tf, tf)
        slot = lax.rem(t, slots)
        return (pltpu.make_async_copy(wg_hbm.at[layer, :, pl.ds(col, tf)], wg_buf.at[slot], w_sems.at[0, slot]),
                pltpu.make_async_copy(wu_hbm.at[layer, :, pl.ds(col, tf)], wu_buf.at[slot], w_sems.at[1, slot]),
                pltpu.make_async_copy(wd_hbm.at[layer, pl.ds(col, tf), :], wd_buf.at[slot], w_sems.at[2, slot]))

    def x_copy(tile):
        return pltpu.make_async_copy(x_hbm.at[pl.ds(pl.multiple_of(tile * tm, tm), tm), :], x_buf, x_sem)

    @pl.when(step == 0)
    def _():
        x_copy(0).start()
        for t in range(per_step):
            for copy in tile_copies(jnp.int32(t)):
                copy.start()

    @pl.when(step + 1 < n_steps)
    def _():
        for k in range(per_step):
            for copy in tile_copies((step + 1) * per_step + k):
                copy.start()

    @pl.when((j == 1) & (i + 1 < n_i))
    def _():
        x_copy(i + 1).start()

    @pl.when(j == 0)
    def _():
        x_copy(i).wait()

    for k in range(per_step):
        for copy in tile_copies(step * per_step + k):
            copy.wait()

    def weights():
        tiles = []
        for k in range(per_step):
            slot = lax.rem(step * per_step + k, slots)
            tiles.append((wg_buf[slot].astype(BF16), wu_buf[slot].astype(BF16), wd_buf[slot].astype(BF16)))
        return tiles

    def partial_out(h, tiles):
        out = None
        for wg, wu, wd in tiles:
            gate = _dot(h, wg)
            up = _dot(h, wu)
            term = _dot((gate * jax.nn.sigmoid(gate) * up).astype(BF16), wd)
            out = term if out is None else out + term
        return out

    @pl.when(j == 0)
    def _():
        tiles = weights()
        for r0 in range(0, tm, first_step_chunk):
            rows = slice(r0, r0 + first_step_chunk)
            x = x_buf[rows, :]
            h = _rms_norm(x, g_ref[...]).astype(BF16)
            h_ref[rows, :] = h
            o_ref[rows, :] = x * (1.0 / FFN_RESIDUAL_WEIGHT) + partial_out(h, tiles)

    @pl.when((j > 0) & (j < last))
    def _():
        o_ref[...] += partial_out(h_ref[...], weights())

    @pl.when(j == last)
    def _():
        o_ref[...] = FFN_RESIDUAL_WEIGHT * (o_ref[...] + partial_out(h_ref[...], weights()))


def _ffn(x, gains, w_gate, w_up, w_down, layer, *, tm, tf=256, tiles_per_step=2):
    n, d = x.shape
    f = w_gate.shape[2]
    n_j = f // (tf * tiles_per_step)
    slots = 2 * tiles_per_step
    assert n % tm == 0 and f % (tf * tiles_per_step) == 0 and n_j >= 2
    assert math.frexp(FFN_RESIDUAL_WEIGHT)[0] == 0.5, "residual weight must be a power of two"
    blocks = tm * d * 4 + d * 4
    temps = (tm * d * 4 + tm * d * 2 + slots * 3 * d * tf * 4
             + tiles_per_step * (3 * d * tf * 2 + 3 * tm * tf * 4))
    return pl.pallas_call(
        functools.partial(_ffn_kernel, layer=layer, first_step_chunk=min(256, tm)),
        out_shape=jax.ShapeDtypeStruct((n, d), F32),
        grid=(n // tm, n_j),
        in_specs=[
            pl.BlockSpec(memory_space=pl.ANY),
            pl.BlockSpec((None, 1, d), lambda i, j: (layer, 0, 0)),
            pl.BlockSpec(memory_space=pl.ANY),
            pl.BlockSpec(memory_space=pl.ANY),
            pl.BlockSpec(memory_space=pl.ANY),
        ],
        out_specs=pl.BlockSpec((tm, d), lambda i, j: (i, 0)),
        scratch_shapes=[pltpu.VMEM((tm, d), F32), pltpu.VMEM((tm, d), BF16),
                        pltpu.VMEM((slots, d, tf), F32), pltpu.VMEM((slots, d, tf), F32),
                        pltpu.VMEM((slots, tf, d), F32), pltpu.SemaphoreType.DMA((3, slots)),
                        pltpu.SemaphoreType.DMA(())],
        compiler_params=pltpu.CompilerParams(
            dimension_semantics=("arbitrary", "arbitrary"),
            vmem_limit_bytes=_vmem_limit(blocks, temps)),
        name="ffn",
    )(x, gains.reshape(gains.shape[0], 1, d), w_gate, w_up, w_down)


def _pool_kernel(x_ref, xh_ref, g_ref, w_ref, s_ref, o_ref, h_ref, p_ref, q_ref, *, tiles_per_batch):
    tm, d = x_ref.shape
    group = d // len(POOL_WINDOWS)
    rows = tm + POOL_HALO
    tile_in_batch = lax.rem(pl.program_id(0), tiles_per_batch)

    gain = g_ref[...]
    halo = _rms_norm(xh_ref[...], gain)
    h_ref[0:POOL_HALO, :] = jnp.where(tile_in_batch == 0, 0.0, halo)
    h_ref[POOL_HALO:, :] = _rms_norm(x_ref[...], gain)

    src = h_ref
    first_exact_row = POOL_FIRST_ROW
    for step in range(len(POOL_WINDOWS)):
        shift = 1 << step
        assert POOL_WINDOWS[step] == 2 * shift and shift <= POOL_FIRST_ROW
        first_exact_row += shift if step else 0
        dst = p_ref if step % 2 == 0 else q_ref
        c0 = step * group
        dst[POOL_FIRST_ROW:, c0:] = (src[POOL_FIRST_ROW:, c0:]
                                     + src[POOL_FIRST_ROW - shift:rows - shift, c0:])
        src = dst
    assert first_exact_row <= POOL_HALO

    pos = tile_in_batch * tm + lax.broadcasted_iota(jnp.int32, (tm, 1), 0)
    for gi, window in enumerate(POOL_WINDOWS):
        cols = slice(gi * group, (gi + 1) * group)
        sums = (p_ref if gi % 2 == 0 else q_ref)[POOL_HALO:, cols]
        inv_count = 1.0 / jnp.minimum(pos + 1, window).astype(F32)
        diff = (sums * inv_count - h_ref[POOL_HALO:, cols]).astype(BF16)
        y = _dot(diff, w_ref[gi].astype(BF16))
        o_ref[:, cols] = x_ref[:, cols] + y * s_ref[:, cols]


def _pool(x, gains, layer, w_groups, scales, mixer, *, seq, tm=512):
    n, d = x.shape
    _, groups, gsz, _ = w_groups.shape
    assert seq % tm == 0 and tm % POOL_HALO == 0 and groups * gsz == d
    halo_blocks = tm // POOL_HALO
    blocks = 2 * tm * d * 4 + POOL_HALO * d * 4 + groups * gsz * gsz * 4 + 2 * d * 4
    temps = 3 * (tm + POOL_HALO) * d * 4 + 2 * tm * d * 4
    return pl.pallas_call(
        functools.partial(_pool_kernel, tiles_per_batch=seq // tm),
        out_shape=jax.ShapeDtypeStruct((n, d), F32),
        grid=(n // tm,),
        in_specs=[
            pl.BlockSpec((tm, d), lambda i: (i, 0)),
            pl.BlockSpec((POOL_HALO, d), lambda i: (jnp.maximum(i * halo_blocks - 1, 0), 0)),
            pl.BlockSpec((None, 1, d), lambda i: (layer, 0, 0)),
            pl.BlockSpec((None, groups, gsz, gsz), lambda i: (mixer, 0, 0, 0)),
            pl.BlockSpec((None, 1, d), lambda i: (mixer, 0, 0)),
        ],
        out_specs=pl.BlockSpec((tm, d), lambda i: (i, 0)),
        scratch_shapes=[pltpu.VMEM((tm + POOL_HALO, d), F32)] * 3,
        compiler_params=pltpu.CompilerParams(
            dimension_semantics=("arbitrary",),
            vmem_limit_bytes=_vmem_limit(blocks, temps)),
        name="pool",
    )(x, x, gains.reshape(gains.shape[0], 1, d), w_groups, scales.reshape(scales.shape[0], 1, d))


def _log_sigmoid(z):
    return jnp.minimum(z, 0.0) - jnp.log1p(jnp.exp(-jnp.abs(z)))


def _fox_proj_kernel(x_ref, g_ref, w_ref, wf_ref, bf_ref, gq_ref, gk_ref, qkv_ref, f_ref, ks_ref,
                     h_ref, carry_ref, fmin_ref, *, tiles_per_batch, qk_tiles, tk, chunk):
    i = pl.program_id(0)
    j = pl.program_id(1)
    tm, tn = qkv_ref.shape
    blocks_per_tile = tm // tk

    def project_qk(is_q):
        acc = _dot_nt(h_ref[...], w_ref[...].astype(BF16))
        gain = jnp.where(is_q, gq_ref[...], gk_ref[...])
        post = jnp.where(is_q, LOG2E / math.sqrt(FOX_HEAD_DIM), 1.0)
        for c0 in range(0, tn, FOX_HEAD_DIM):
            y = _rms_norm(acc[:, c0:c0 + FOX_HEAD_DIM], gain) * post
            qkv_ref[:, c0:c0 + FOX_HEAD_DIM] = y.astype(BF16)

    @pl.when(j == 0)
    def _():
        h_ref[...] = _rms_norm(x_ref[...], g_ref[...]).astype(BF16)
        tile_in_batch = lax.rem(i, tiles_per_batch)

        @pl.when(tile_in_batch == 0)
        def _():
            carry_ref[...] = jnp.zeros_like(carry_ref)
            fmin_ref[...] = jnp.zeros_like(fmin_ref)

        r = lax.broadcasted_iota(jnp.int32, (chunk, chunk), 0)
        c = lax.broadcasted_iota(jnp.int32, (chunk, chunk), 1)
        tri = (c <= r).astype(BF16)
        carry = carry_ref[...]
        block_max = [None] * blocks_per_tile
        block_min = [None] * blocks_per_tile
        for c0 in range(0, tm, chunk):
            logit = _dot_nt(h_ref[c0:c0 + chunk, :], wf_ref[...]) + bf_ref[...]
            prefix = sum(_dot(tri, t) for t in _split_bf16(_log_sigmoid(logit)))
            total = prefix + carry
            carry = total[chunk - 1:chunk, :]
            f2 = total * LOG2E
            for t, term in enumerate(_split_bf16(f2)):
                f_ref[c0:c0 + chunk, t * V7X_LANES:(t + 1) * V7X_LANES] = term
            blk = c0 // tk
            hi = jnp.max(f2, axis=0, keepdims=True)
            lo = jnp.min(f2, axis=0, keepdims=True)
            block_max[blk] = hi if block_max[blk] is None else jnp.maximum(block_max[blk], hi)
            block_min[blk] = lo if block_min[blk] is None else jnp.minimum(block_min[blk], lo)
        carry_ref[...] = carry

        qk_bound = (LOG2E * math.sqrt(FOX_HEAD_DIM) * QK_BOUND_SLACK
                    * jnp.max(jnp.abs(gq_ref[...]), axis=-1, keepdims=True)
                    * jnp.max(jnp.abs(gk_ref[...]), axis=-1, keepdims=True))
        n_blocks = fmin_ref.shape[0]
        row = lax.broadcasted_iota(jnp.int32, fmin_ref.shape, 0)
        ks_ref[...] = jnp.zeros_like(ks_ref)
        for blk in range(blocks_per_tile):
            qb = tile_in_batch * blocks_per_tile + blk
            fmin_ref[pl.ds(qb, 1), :] = block_min[blk]
            gap = 2.0 * qk_bound + block_max[blk] - fmin_ref[...]
            needed = (gap > -SKIP_LOG2_MARGIN) | (row >= qb)
            ks_ref[blk:blk + 1, :] = jnp.min(jnp.where(needed, row, n_blocks), axis=0, keepdims=True)

        project_qk(True)

    @pl.when((j > 0) & (j < qk_tiles))
    def _():
        project_qk(j < qk_tiles // 2)

    @pl.when(j >= qk_tiles)
    def _():
        qkv_ref[...] = _dot_nt(h_ref[...], w_ref[...].astype(BF16)).astype(BF16)


def _fox_proj(x, gains, layer, w_in_t, w_f_t, b_f, q_gains, k_gains, mixer, *, seq, tm, tk, tn, chunk=256):
    n, d = x.shape
    assert seq % tm == 0 and tm % tk == 0 and tk % chunk == 0 and d % tn == 0 and tn % FOX_HEAD_DIM == 0
    assert tm // tk <= V7X_SUBLANES
    qk_tiles = 2 * d // tn
    f_cols = F_TERMS * V7X_LANES
    blocks = tm * d * 4 + d * tn * 4 + d * V7X_LANES * 2 + tm * tn * 2 + tm * f_cols * 2
    temps = tm * d * 2 + d * tn * 2 + 3 * tm * tn * 4 + 4 * chunk * chunk * 4
    return pl.pallas_call(
        functools.partial(_fox_proj_kernel, tiles_per_batch=seq // tm, qk_tiles=qk_tiles, tk=tk, chunk=chunk),
        out_shape=(jax.ShapeDtypeStruct((n, 3 * d), BF16),
                   jax.ShapeDtypeStruct((n, f_cols), BF16),
                   jax.ShapeDtypeStruct((n // tm, V7X_SUBLANES, V7X_LANES), jnp.int32)),
        grid=(n // tm, 3 * d // tn),
        in_specs=[
            pl.BlockSpec((tm, d), lambda i, j: (i, 0)),
            pl.BlockSpec((None, 1, d), lambda i, j: (layer, 0, 0)),
            pl.BlockSpec((None, tn, d), lambda i, j: (mixer, j, 0)),
            pl.BlockSpec((None, V7X_LANES, d), lambda i, j: (mixer, 0, 0)),
            pl.BlockSpec((None, 1, V7X_LANES), lambda i, j: (mixer, 0, 0)),
            pl.BlockSpec((None, 1, FOX_HEAD_DIM), lambda i, j: (mixer, 0, 0)),
            pl.BlockSpec((None, 1, FOX_HEAD_DIM), lambda i, j: (mixer, 0, 0)),
        ],
        out_specs=(pl.BlockSpec((tm, tn), lambda i, j: (i, j)),
                   pl.BlockSpec((tm, f_cols), lambda i, j: (i, 0)),
                   pl.BlockSpec((None, V7X_SUBLANES, V7X_LANES), lambda i, j: (i, 0, 0))),
        scratch_shapes=[pltpu.VMEM((tm, d), BF16), pltpu.VMEM((1, V7X_LANES), F32),
                        pltpu.VMEM((seq // tk, V7X_LANES), F32)],
        compiler_params=pltpu.CompilerParams(
            dimension_semantics=("arbitrary", "arbitrary"),
            vmem_limit_bytes=_vmem_limit(blocks, temps)),
        name="fox_proj",
    )(x, gains.reshape(gains.shape[0], 1, d), w_in_t, w_f_t, b_f,
      q_gains.reshape(q_gains.shape[0], 1, -1), k_gains.reshape(k_gains.shape[0], 1, -1))


def _fox_attn_kernel(first_ref, q_ref, k_ref, v_ref, f_ref, o_ref, kaug_ref, qaug_ref, s_ref, cmax_ref, m_ref,
                     l_ref, acc_ref, *, tq, tk):
    head = pl.program_id(1)
    seq = k_ref.shape[0]
    nq = seq // tq
    assert tq == tk
    table_base = (pl.program_id(0) * pl.num_programs(1) + head) * nq

    r = lax.broadcasted_iota(jnp.int32, (F_TERMS * V7X_LANES, V7X_LANES), 0)
    c = lax.broadcasted_iota(jnp.int32, (F_TERMS * V7X_LANES, V7X_LANES), 1)
    sel_q = (r == c * V7X_LANES + head).astype(BF16)
    sel_k = -(r == (c - F_TERMS) * V7X_LANES + head).astype(BF16)
    lane = lax.broadcasted_iota(jnp.int32, (1, V7X_LANES), 1)
    ones_q = ((lane >= F_TERMS) & (lane < 2 * F_TERMS)).astype(F32)
    ones_k = (lane < F_TERMS).astype(F32)

    kaug_ref[:, 0:FOX_HEAD_DIM] = k_ref[...]
    kaug_ref[:, FOX_HEAD_DIM:] = (_dot(f_ref[...], sel_k) + ones_k).astype(BF16)

    def rows_of(block, n_blocks):
        return pl.ds(pl.multiple_of(block * tk, tk), n_blocks * tk)

    def build_q(qi):
        qaug_ref[:, 0:FOX_HEAD_DIM] = q_ref[rows_of(qi, 1), :]
        qaug_ref[:, FOX_HEAD_DIM:] = (_dot(f_ref[rows_of(qi, 1), :], sel_q) + ones_q).astype(BF16)

    def logits(rows):
        return _dot_nt(kaug_ref[rows, :], qaug_ref[...])

    def tail_scores(qi, with_prev):
        s = logits(rows_of(qi - 1, 2) if with_prev else rows_of(qi, 1))
        key = lax.broadcasted_iota(jnp.int32, (tk, tq), 0)
        qry = lax.broadcasted_iota(jnp.int32, (tk, tq), 1)
        diag = jnp.where(key <= qry, s[-tk:], NEG_LARGE)
        cmax = jnp.max(diag, axis=0, keepdims=True)
        if with_prev:
            cmax = jnp.maximum(cmax, jnp.max(s[:tk], axis=0, keepdims=True))
            s_ref[0:tk, :] = s[:tk]
            s_ref[tk:, :] = diag
        else:
            s_ref[0:tk, :] = diag
        cmax_ref[...] = cmax

    def online_update(s, block_max, v_rows):
        m_old = m_ref[...]
        m_new = jnp.maximum(m_old, block_max)
        alpha = jnp.exp2(m_old - m_new)
        p = jnp.exp2(s - m_new)
        l_ref[...] = alpha * l_ref[...] + jnp.sum(p, axis=0, keepdims=True)
        pv = lax.dot_general(v_ref[v_rows, :], p.astype(BF16), (((0,), (0,)), ((), ())),
                             preferred_element_type=F32)
        acc_ref[...] = alpha * acc_ref[...] + pv
        m_ref[...] = m_new

    def tail_accumulate(qi, with_prev):
        if with_prev:
            online_update(s_ref[...], cmax_ref[...], rows_of(qi - 1, 2))
        else:
            online_update(s_ref[0:tk, :], cmax_ref[...], rows_of(qi, 1))

    def start_block(qi):
        m_ref[...] = jnp.full_like(m_ref, NEG_LARGE)
        l_ref[...] = jnp.zeros_like(l_ref)
        acc_ref[...] = jnp.zeros_like(acc_ref)

        def body(kb, c):
            s = logits(rows_of(kb, 1))
            online_update(s, jnp.max(s, axis=0, keepdims=True), rows_of(kb, 1))
            return c
        lax.fori_loop(first_ref[table_base + qi], qi - 1, body, 0)

    def finish_block(qi):
        o_ref[rows_of(qi, 1), :] = (acc_ref[...] / l_ref[...]).T.astype(BF16)

    build_q(0)
    tail_scores(0, False)
    start_block(0)
    build_q(1)
    tail_accumulate(0, False)
    tail_scores(1, True)
    finish_block(0)

    def query_block(qi, carry):
        start_block(qi)
        build_q(qi + 1)
        tail_accumulate(qi, True)
        tail_scores(qi + 1, True)
        finish_block(qi)
        return carry
    lax.fori_loop(1, nq - 1, query_block, 0)

    start_block(nq - 1)
    tail_accumulate(nq - 1, True)
    finish_block(nq - 1)


def _fox_attn(first_block, qkv, f_terms, *, batch, seq, heads, tq):
    n = qkv.shape[0]
    d = heads * FOX_HEAD_DIM
    nq = seq // tq
    f_cols = f_terms.shape[1]
    assert seq % tq == 0 and nq >= 2 and qkv.shape == (n, 3 * d) and first_block.shape == (batch * heads * nq,)
    blocks = 4 * seq * FOX_HEAD_DIM * 2 + seq * f_cols * 2
    temps = (seq + tq) * 2 * FOX_HEAD_DIM * 2 + tq * FOX_HEAD_DIM * 4 + 10 * tq * tq * 4
    return pl.pallas_call(
        functools.partial(_fox_attn_kernel, tq=tq, tk=tq),
        out_shape=jax.ShapeDtypeStruct((n, d), BF16),
        grid_spec=pltpu.PrefetchScalarGridSpec(
            num_scalar_prefetch=1,
            grid=(batch, heads),
            in_specs=[
                pl.BlockSpec((seq, FOX_HEAD_DIM), lambda b, h, first: (b, h)),
                pl.BlockSpec((seq, FOX_HEAD_DIM), lambda b, h, first: (b, heads + h)),
                pl.BlockSpec((seq, FOX_HEAD_DIM), lambda b, h, first: (b, 2 * heads + h)),
                pl.BlockSpec((seq, f_cols), lambda b, h, first: (b, 0)),
            ],
            out_specs=pl.BlockSpec((seq, FOX_HEAD_DIM), lambda b, h, first: (b, h)),
            scratch_shapes=[pltpu.VMEM((seq, 2 * FOX_HEAD_DIM), BF16), pltpu.VMEM((tq, 2 * FOX_HEAD_DIM), BF16),
                            pltpu.VMEM((2 * tq, tq), F32), pltpu.VMEM((1, tq), F32), pltpu.VMEM((1, tq), F32),
                            pltpu.VMEM((1, tq), F32), pltpu.VMEM((FOX_HEAD_DIM, tq), F32)]),
        compiler_params=pltpu.CompilerParams(
            dimension_semantics=("arbitrary", "arbitrary"),
            vmem_limit_bytes=_vmem_limit(blocks, temps)),
        name="fox_attn",
    )(first_block, qkv, qkv, qkv, f_terms)


def _fox_out_kernel(o_ref, w_ref, x_ref, y_ref):
    y_ref[...] = x_ref[...] + _dot(o_ref[...], w_ref[...].astype(BF16))


def _fox_out(o, w_out, mixer, x, *, tm, tn):
    n, d = x.shape
    assert n % tm == 0 and d % tn == 0
    blocks = tm * d * 2 + d * tn * 4 + 2 * tm * tn * 4
    temps = d * tn * 2 + tm * tn * 4
    return pl.pallas_call(
        _fox_out_kernel,
        out_shape=jax.ShapeDtypeStruct((n, d), F32),
        grid=(d // tn, n // tm),
        in_specs=[
            pl.BlockSpec((tm, d), lambda j, i: (i, 0)),
            pl.BlockSpec((None, d, tn), lambda j, i: (mixer, 0, j)),
            pl.BlockSpec((tm, tn), lambda j, i: (i, j)),
        ],
        out_specs=pl.BlockSpec((tm, tn), lambda j, i: (i, j)),
        compiler_params=pltpu.CompilerParams(
            dimension_semantics=("arbitrary", "arbitrary"),
            vmem_limit_bytes=_vmem_limit(blocks, temps)),
        name="fox_out",
    )(o, w_out, x)


def _fox(x, gains, layer, w_in_t, w_f_t, b_f, q_gains, k_gains, w_out, mixer, *, batch, seq, tm, tq):
    heads = x.shape[1] // FOX_HEAD_DIM
    qkv, f_terms, first = _fox_proj(x, gains, layer, w_in_t, w_f_t, b_f, q_gains, k_gains, mixer,
                                    seq=seq, tm=tm, tk=tq, tn=min(1024, x.shape[1]))
    first = first[:, :tm // tq, :heads].reshape(batch, seq // tq, heads).transpose(0, 2, 1).reshape(-1)
    o = _fox_attn(first, qkv, f_terms, batch=batch, seq=seq, heads=heads, tq=tq)
    return _fox_out(o, w_out, mixer, x, tm=tm, tn=min(1024, x.shape[1]))


def kernel(x, ffn1_norm, ffn1_w_gate, ffn1_w_up, ffn1_w_down, mix_norm, pool_w, pool_scale, fox_w_in, fox_b_f,
           fox_q_gain, fox_k_gain, fox_w_out, ffn2_norm, ffn2_w_gate, ffn2_w_up, ffn2_w_down):
    batch, seq, d = x.shape
    depth = ffn1_norm.shape[0]
    n_mixers = 2
    heads = d // FOX_HEAD_DIM
    tm = min(1024, seq)
    tq = min(512, seq)
    assert heads <= V7X_LANES

    w_in_t = jnp.swapaxes(fox_w_in, 1, 2)
    w_f_t = jnp.pad(w_in_t[:, 3 * d:, :], ((0, 0), (0, V7X_LANES - heads), (0, 0))).astype(BF16)
    b_f = jnp.pad(fox_b_f, ((0, 0), (0, V7X_LANES - heads))).reshape(-1, 1, V7X_LANES)

    h = x.reshape(batch * seq, d)
    for i in range(depth):
        h = _ffn(h, ffn1_norm, ffn1_w_gate, ffn1_w_up, ffn1_w_down, i, tm=tm)
        j = i // n_mixers
        if i % n_mixers == 0:
            h = _pool(h, mix_norm, i, pool_w, pool_scale, j, seq=seq, tm=min(512, seq))
        else:
            h = _fox(h, mix_norm, i, w_in_t, w_f_t, b_f, fox_q_gain, fox_k_gain, fox_w_out, j,
                     batch=batch, seq=seq, tm=tm, tq=tq)
        h = _ffn(h, ffn2_norm, ffn2_w_gate, ffn2_w_up, ffn2_w_down, i, tm=tm)
    return h.reshape(batch, seq, d)
```

```python
import functools
import math

import jax
import jax.numpy as jnp
from jax import lax
from jax.experimental import pallas as pl
from jax.experimental.pallas import tpu as pltpu

RMS_EPS = 1e-6
FFN_RESIDUAL_WEIGHT = 0.5
POOL_WINDOWS = (2, 4, 8, 16)
FOX_HEAD_DIM = 128
NEG_LARGE = -1e30
LOG2E = math.log2(math.e)

SKIP_LOG2_MARGIN = 160.0
QK_BOUND_SLACK = 1.01

V7X_LANES = 128
V7X_SUBLANES = 8
V7X_VMEM_BYTES = 64 * 1024 * 1024
POOL_HALO = 32
POOL_FIRST_ROW = 8
F_TERMS = 3

BF16 = jnp.bfloat16
F32 = jnp.float32


def _vmem_limit(block_bytes, temp_bytes):
    return min(2 * block_bytes + temp_bytes + (4 << 20), V7X_VMEM_BYTES - (2 << 20))


def _rms_norm(x, gain):
    ms = jnp.mean(x * x, axis=-1, keepdims=True)
    return x * lax.rsqrt(ms + RMS_EPS) * gain


def _dot(a, b):
    return jnp.dot(a, b, preferred_element_type=F32)


def _dot_nt(a, b):
    return lax.dot_general(a, b, (((1,), (1,)), ((), ())), preferred_element_type=F32)


def _split_bf16(x):
    terms = []
    for _ in range(F_TERMS):
        t = x.astype(BF16)
        terms.append(t)
        x = x - t.astype(F32)
    return terms


def _ffn_kernel(x_hbm, g_ref, wg_hbm, wu_hbm, wd_hbm, o_ref, x_buf, h_ref, wg_buf, wu_buf, wd_buf, w_sems, x_sem,
                *, layer, first_step_chunk):
    i = pl.program_id(0)
    j = pl.program_id(1)
    n_i = pl.num_programs(0)
    n_j = pl.num_programs(1)
    last = n_j - 1
    tm = x_buf.shape[0]
    slots, _, tf = wg_buf.shape
    per_step = slots // 2
    step = i * n_j + j
    n_steps = n_i * n_j

    def tile_copies(t):
        col = pl.multiple_of(lax.rem(t, n_j * per_step) * tf, tf)
        slot = lax.rem(t, slots)
        return (pltpu.make_async_copy(wg_hbm.at[layer, :, pl.ds(col, tf)], wg_buf.at[slot], w_sems.at[0, slot]),
                pltpu.make_async_copy(wu_hbm.at[layer, :, pl.ds(col, tf)], wu_buf.at[slot], w_sems.at[1, slot]),
                pltpu.make_async_copy(wd_hbm.at[layer, pl.ds(col, tf), :], wd_buf.at[slot], w_sems.at[2, slot]))

    def x_copy(tile):
        return pltpu.make_async_copy(x_hbm.at[pl.ds(pl.multiple_of(tile * tm, tm), tm), :], x_buf, x_sem)

    @pl.when(step == 0)
    def _():
        x_copy(0).start()
        for t in range(per_step):
            for copy in tile_copies(jnp.int32(t)):
                copy.start()

    @pl.when(step + 1 < n_steps)
    def _():
        for k in range(per_step):
            for copy in tile_copies((step + 1) * per_step + k):
                copy.start()

    @pl.when((j == 1) & (i + 1 < n_i))
    def _():
        x_copy(i + 1).start()

    @pl.when(j == 0)
    def _():
        x_copy(i).wait()

    for k in range(per_step):
        for copy in tile_copies(step * per_step + k):
            copy.wait()

    def weights():
        tiles = []
        for k in range(per_step):
            slot = lax.rem(step * per_step + k, slots)
            tiles.append((wg_buf[slot].astype(BF16), wu_buf[slot].astype(BF16), wd_buf[slot].astype(BF16)))
        return tiles

    def partial_out(h, tiles):
        out = None
        for wg, wu, wd in tiles:
            gate = _dot(h, wg)
            up = _dot(h, wu)
            term = _dot((gate * jax.nn.sigmoid(gate) * up).astype(BF16), wd)
            out = term if out is None else out + term
        return out

    @pl.when(j == 0)
    def _():
        tiles = weights()
        for r0 in range(0, tm, first_step_chunk):
            rows = slice(r0, r0 + first_step_chunk)
            x = x_buf[rows, :]
            h = _rms_norm(x, g_ref[...]).astype(BF16)
            h_ref[rows, :] = h
            o_ref[rows, :] = x * (1.0 / FFN_RESIDUAL_WEIGHT) + partial_out(h, tiles)

    @pl.when((j > 0) & (j < last))
    def _():
        o_ref[...] += partial_out(h_ref[...], weights())

    @pl.when(j == last)
    def _():
        o_ref[...] = FFN_RESIDUAL_WEIGHT * (o_ref[...] + partial_out(h_ref[...], weights()))


def _ffn(x, gains, w_gate, w_up, w_down, layer, *, tm, tf=256, tiles_per_step=2):
    n, d = x.shape
    f = w_gate.shape[2]
    n_j = f // (tf * tiles_per_step)
    slots = 2 * tiles_per_step
    assert n % tm == 0 and f % (tf * tiles_per_step) == 0 and n_j >= 2
    assert math.frexp(FFN_RESIDUAL_WEIGHT)[0] == 0.5, "residual weight must be a power of two"
    blocks = tm * d * 4 + d * 4
    temps = (tm * d * 4 + tm * d * 2 + slots * 3 * d * tf * 4
             + tiles_per_step * (3 * d * tf * 2 + 3 * tm * tf * 4))
    return pl.pallas_call(
        functools.partial(_ffn_kernel, layer=layer, first_step_chunk=min(256, tm)),
        out_shape=jax.ShapeDtypeStruct((n, d), F32),
        grid=(n // tm, n_j),
        in_specs=[
            pl.BlockSpec(memory_space=pl.ANY),
            pl.BlockSpec((None, 1, d), lambda i, j: (layer, 0, 0)),
            pl.BlockSpec(memory_space=pl.ANY),
            pl.BlockSpec(memory_space=pl.ANY),
            pl.BlockSpec(memory_space=pl.ANY),
        ],
        out_specs=pl.BlockSpec((tm, d), lambda i, j: (i, 0)),
        scratch_shapes=[pltpu.VMEM((tm, d), F32), pltpu.VMEM((tm, d), BF16),
                        pltpu.VMEM((slots, d, tf), F32), pltpu.VMEM((slots, d, tf), F32),
                        pltpu.VMEM((slots, tf, d), F32), pltpu.SemaphoreType.DMA((3, slots)),
                        pltpu.SemaphoreType.DMA(())],
        compiler_params=pltpu.CompilerParams(
            dimension_semantics=("arbitrary", "arbitrary"),
            vmem_limit_bytes=_vmem_limit(blocks, temps)),
        name="ffn",
    )(x, gains.reshape(gains.shape[0], 1, d), w_gate, w_up, w_down)


def _pool_kernel(x_ref, xh_ref, g_ref, w_ref, s_ref, o_ref, h_ref, p_ref, q_ref, *, tiles_per_batch):
    tm, d = x_ref.shape
    group = d // len(POOL_WINDOWS)
    rows = tm + POOL_HALO
    tile_in_batch = lax.rem(pl.program_id(0), tiles_per_batch)

    gain = g_ref[...]
    halo = _rms_norm(xh_ref[...], gain)
    h_ref[0:POOL_HALO, :] = jnp.where(tile_in_batch == 0, 0.0, halo)
    h_ref[POOL_HALO:, :] = _rms_norm(x_ref[...], gain)

    src = h_ref
    first_exact_row = POOL_FIRST_ROW
    for step in range(len(POOL_WINDOWS)):
        shift = 1 << step
        assert POOL_WINDOWS[step] == 2 * shift and shift <= POOL_FIRST_ROW
        first_exact_row += shift if step else 0
        dst = p_ref if step % 2 == 0 else q_ref
        c0 = step * group
        dst[POOL_FIRST_ROW:, c0:] = (src[POOL_FIRST_ROW:, c0:]
                                     + src[POOL_FIRST_ROW - shift:rows - shift, c0:])
        src = dst
    assert first_exact_row <= POOL_HALO

    pos = tile_in_batch * tm + lax.broadcasted_iota(jnp.int32, (tm, 1), 0)
    for gi, window in enumerate(POOL_WINDOWS):
        cols = slice(gi * group, (gi + 1) * group)
        sums = (p_ref if gi % 2 == 0 else q_ref)[POOL_HALO:, cols]
        inv_count = 1.0 / jnp.minimum(pos + 1, window).astype(F32)
        diff = (sums * inv_count - h_ref[POOL_HALO:, cols]).astype(BF16)
        y = _dot(diff, w_ref[gi].astype(BF16))
        o_ref[:, cols] = x_ref[:, cols] + y * s_ref[:, cols]


def _pool(x, gains, layer, w_groups, scales, mixer, *, seq, tm=512):
    n, d = x.shape
    _, groups, gsz, _ = w_groups.shape
    assert seq % tm == 0 and tm % POOL_HALO == 0 and groups * gsz == d
    halo_blocks = tm // POOL_HALO
    blocks = 2 * tm * d * 4 + POOL_HALO * d * 4 + groups * gsz * gsz * 4 + 2 * d * 4
    temps = 3 * (tm + POOL_HALO) * d * 4 + 2 * tm * d * 4
    return pl.pallas_call(
        functools.partial(_pool_kernel, tiles_per_batch=seq // tm),
        out_shape=jax.ShapeDtypeStruct((n, d), F32),
        grid=(n // tm,),
        in_specs=[
            pl.BlockSpec((tm, d), lambda i: (i, 0)),
            pl.BlockSpec((POOL_HALO, d), lambda i: (jnp.maximum(i * halo_blocks - 1, 0), 0)),
            pl.BlockSpec((None, 1, d), lambda i: (layer, 0, 0)),
            pl.BlockSpec((None, groups, gsz, gsz), lambda i: (mixer, 0, 0, 0)),
            pl.BlockSpec((None, 1, d), lambda i: (mixer, 0, 0)),
        ],
        out_specs=pl.BlockSpec((tm, d), lambda i: (i, 0)),
        scratch_shapes=[pltpu.VMEM((tm + POOL_HALO, d), F32)] * 3,
        compiler_params=pltpu.CompilerParams(
            dimension_semantics=("arbitrary",),
            vmem_limit_bytes=_vmem_limit(blocks, temps)),
        name="pool",
    )(x, x, gains.reshape(gains.shape[0], 1, d), w_groups, scales.reshape(scales.shape[0], 1, d))


def _log_sigmoid(z):
    return jnp.minimum(z, 0.0) - jnp.log1p(jnp.exp(-jnp.abs(z)))


def _fox_proj_kernel(x_ref, g_ref, w_ref, wf_ref, bf_ref, gq_ref, gk_ref, qkv_ref, f_ref, ks_ref,
                     h_ref, carry_ref, fmin_ref, *, tiles_per_batch, qk_tiles, tk, chunk):
    i = pl.program_id(0)
    j = pl.program_id(1)
    tm, tn = qkv_ref.shape
    blocks_per_tile = tm // tk

    def project_qk(is_q, rows, w):
        acc = _dot_nt(h_ref[rows, :], w)
        gain = jnp.where(is_q, gq_ref[...], gk_ref[...])
        post = jnp.where(is_q, LOG2E / math.sqrt(FOX_HEAD_DIM), 1.0)
        for c0 in range(0, tn, FOX_HEAD_DIM):
            y = _rms_norm(acc[:, c0:c0 + FOX_HEAD_DIM], gain) * post
            qkv_ref[rows, c0:c0 + FOX_HEAD_DIM] = y.astype(BF16)

    @pl.when(j == 0)
    def _():
        tile_in_batch = lax.rem(i, tiles_per_batch)

        @pl.when(tile_in_batch == 0)
        def _():
            carry_ref[...] = jnp.zeros_like(carry_ref)
            fmin_ref[...] = jnp.zeros_like(fmin_ref)

        for c0 in range(0, tm, chunk):
            h_ref[c0:c0 + chunk, :] = _rms_norm(x_ref[c0:c0 + chunk, :], g_ref[...]).astype(BF16)
        log_f_terms = _split_bf16(_log_sigmoid(_dot_nt(h_ref[...], wf_ref[...]) + bf_ref[...]))
        project_qk(True, slice(None), w_ref[...].astype(BF16))

        r = lax.broadcasted_iota(jnp.int32, (chunk, chunk), 0)
        c = lax.broadcasted_iota(jnp.int32, (chunk, chunk), 1)
        tri = (c <= r).astype(BF16)
        carry = carry_ref[...]
        block_max = [None] * blocks_per_tile
        block_min = [None] * blocks_per_tile
        for c0 in range(0, tm, chunk):
            prefix = sum(_dot(tri, t[c0:c0 + chunk, :]) for t in log_f_terms)
            total = prefix + carry
            carry = total[chunk - 1:chunk, :]
            f2 = total * LOG2E
            for t, term in enumerate(_split_bf16(f2)):
                f_ref[c0:c0 + chunk, t * V7X_LANES:(t + 1) * V7X_LANES] = term
            blk = c0 // tk
            hi = jnp.max(f2, axis=0, keepdims=True)
            lo = jnp.min(f2, axis=0, keepdims=True)
            block_max[blk] = hi if block_max[blk] is None else jnp.maximum(block_max[blk], hi)
            block_min[blk] = lo if block_min[blk] is None else jnp.minimum(block_min[blk], lo)
        carry_ref[...] = carry

        qk_bound = (LOG2E * math.sqrt(FOX_HEAD_DIM) * QK_BOUND_SLACK
                    * jnp.max(jnp.abs(gq_ref[...]), axis=-1, keepdims=True)
                    * jnp.max(jnp.abs(gk_ref[...]), axis=-1, keepdims=True))
        n_blocks = fmin_ref.shape[0]
        row = lax.broadcasted_iota(jnp.int32, fmin_ref.shape, 0)
        ks_ref[...] = jnp.zeros_like(ks_ref)
        for blk in range(blocks_per_tile):
            qb = tile_in_batch * blocks_per_tile + blk
            fmin_ref[pl.ds(qb, 1), :] = block_min[blk]
            gap = 2.0 * qk_bound + block_max[blk] - fmin_ref[...]
            needed = (gap > -SKIP_LOG2_MARGIN) | (row >= qb)
            ks_ref[blk:blk + 1, :] = jnp.min(jnp.where(needed, row, n_blocks), axis=0, keepdims=True)

    @pl.when((j > 0) & (j < qk_tiles))
    def _():
        w = w_ref[...].astype(BF16)
        for r0 in range(0, tm, tm // 2):
            project_qk(j < qk_tiles // 2, slice(r0, r0 + tm // 2), w)

    @pl.when(j >= qk_tiles)
    def _():
        qkv_ref[...] = _dot_nt(h_ref[...], w_ref[...].astype(BF16)).astype(BF16)


def _fox_proj(x, gains, layer, w_in_t, w_f_t, b_f, q_gains, k_gains, mixer, *, seq, tm, tk, tn, chunk=256):
    n, d = x.shape
    assert seq % tm == 0 and tm % tk == 0 and tk % chunk == 0 and d % tn == 0 and tn % FOX_HEAD_DIM == 0
    assert tm // tk <= V7X_SUBLANES
    qk_tiles = 2 * d // tn
    f_cols = F_TERMS * V7X_LANES
    blocks = tm * d * 4 + d * tn * 4 + d * V7X_LANES * 2 + tm * tn * 2 + tm * f_cols * 2
    temps = tm * d * 2 + d * tn * 2 + 3 * tm * tn * 4 + 4 * chunk * chunk * 4
    return pl.pallas_call(
        functools.partial(_fox_proj_kernel, tiles_per_batch=seq // tm, qk_tiles=qk_tiles, tk=tk, chunk=chunk),
        out_shape=(jax.ShapeDtypeStruct((n, 3 * d), BF16),
                   jax.ShapeDtypeStruct((n, f_cols), BF16),
                   jax.ShapeDtypeStruct((n // tm, V7X_SUBLANES, V7X_LANES), jnp.int32)),
        grid=(n // tm, 3 * d // tn),
        in_specs=[
            pl.BlockSpec((tm, d), lambda i, j: (i, 0)),
            pl.BlockSpec((None, 1, d), lambda i, j: (layer, 0, 0)),
            pl.BlockSpec((None, tn, d), lambda i, j: (mixer, j, 0)),
            pl.BlockSpec((None, V7X_LANES, d), lambda i, j: (mixer, 0, 0)),
            pl.BlockSpec((None, 1, V7X_LANES), lambda i, j: (mixer, 0, 0)),
            pl.BlockSpec((None, 1, FOX_HEAD_DIM), lambda i, j: (mixer, 0, 0)),
            pl.BlockSpec((None, 1, FOX_HEAD_DIM), lambda i, j: (mixer, 0, 0)),
        ],
        out_specs=(pl.BlockSpec((tm, tn), lambda i, j: (i, j)),
                   pl.BlockSpec((tm, f_cols), lambda i, j: (i, 0)),
                   pl.BlockSpec((None, V7X_SUBLANES, V7X_LANES), lambda i, j: (i, 0, 0))),
        scratch_shapes=[pltpu.VMEM((tm, d), BF16), pltpu.VMEM((1, V7X_LANES), F32),
                        pltpu.VMEM((seq // tk, V7X_LANES), F32)],
        compiler_params=pltpu.CompilerParams(
            dimension_semantics=("arbitrary", "arbitrary"),
            vmem_limit_bytes=_vmem_limit(blocks, temps)),
        name="fox_proj",
    )(x, gains.reshape(gains.shape[0], 1, d), w_in_t, w_f_t, b_f,
      q_gains.reshape(q_gains.shape[0], 1, -1), k_gains.reshape(k_gains.shape[0], 1, -1))


def _fox_attn_kernel(first_ref, q_ref, k_ref, v_ref, f_ref, o_ref, kaug_ref, qaug_ref, s_ref, cmax_ref, m_ref,
                     l_ref, acc_ref, *, tq, tk):
    head = pl.program_id(1)
    seq = k_ref.shape[0]
    nq = seq // tq
    assert tq == tk
    table_base = (pl.program_id(0) * pl.num_programs(1) + head) * nq

    r = lax.broadcasted_iota(jnp.int32, (F_TERMS * V7X_LANES, V7X_LANES), 0)
    c = lax.broadcasted_iota(jnp.int32, (F_TERMS * V7X_LANES, V7X_LANES), 1)
    sel_q = (r == c * V7X_LANES + head).astype(BF16)
    sel_k = -(r == (c - F_TERMS) * V7X_LANES + head).astype(BF16)
    lane = lax.broadcasted_iota(jnp.int32, (1, V7X_LANES), 1)
    ones_q = ((lane >= F_TERMS) & (lane < 2 * F_TERMS)).astype(F32)
    ones_k = (lane < F_TERMS).astype(F32)

    kaug_ref[:, 0:FOX_HEAD_DIM] = k_ref[...]
    kaug_ref[:, FOX_HEAD_DIM:] = (_dot(f_ref[...], sel_k) + ones_k).astype(BF16)

    def rows_of(block, n_blocks):
        return pl.ds(pl.multiple_of(block * tk, tk), n_blocks * tk)

    def build_q(qi):
        qaug_ref[:, 0:FOX_HEAD_DIM] = q_ref[rows_of(qi, 1), :]
        qaug_ref[:, FOX_HEAD_DIM:] = (_dot(f_ref[rows_of(qi, 1), :], sel_q) + ones_q).astype(BF16)

    def logits(rows):
        return _dot_nt(kaug_ref[rows, :], qaug_ref[...])

    def tail_scores(qi, with_prev):
        s = logits(rows_of(qi - 1, 2) if with_prev else rows_of(qi, 1))
        key = lax.broadcasted_iota(jnp.int32, (tk, tq), 0)
        qry = lax.broadcasted_iota(jnp.int32, (tk, tq), 1)
        diag = jnp.where(key <= qry, s[-tk:], NEG_LARGE)
        cmax = jnp.max(diag, axis=0, keepdims=True)
        if with_prev:
            cmax = jnp.maximum(cmax, jnp.max(s[:tk], axis=0, keepdims=True))
            s_ref[0:tk, :] = s[:tk]
            s_ref[tk:, :] = diag
        else:
            s_ref[0:tk, :] = diag
        cmax_ref[...] = cmax

    def online_update(s, block_max, v_rows):
        m_old = m_ref[...]
        m_new = jnp.maximum(m_old, block_max)
        alpha = jnp.exp2(m_old - m_new)
        p = jnp.exp2(s - m_new)
        l_ref[...] = alpha * l_ref[...] + jnp.sum(p, axis=0, keepdims=True)
        pv = lax.dot_general(v_ref[v_rows, :], p.astype(BF16), (((0,), (0,)), ((), ())),
                             preferred_element_type=F32)
        acc_ref[...] = alpha * acc_ref[...] + pv
        m_ref[...] = m_new

    def tail_accumulate(qi, with_prev):
        if with_prev:
            online_update(s_ref[...], cmax_ref[...], rows_of(qi - 1, 2))
        else:
            online_update(s_ref[0:tk, :], cmax_ref[...], rows_of(qi, 1))

    def start_block(qi):
        m_ref[...] = jnp.full_like(m_ref, NEG_LARGE)
        l_ref[...] = jnp.zeros_like(l_ref)
        acc_ref[...] = jnp.zeros_like(acc_ref)

        def body(kb, c):
            s = logits(rows_of(kb, 1))
            online_update(s, jnp.max(s, axis=0, keepdims=True), rows_of(kb, 1))
            return c
        lax.fori_loop(first_ref[table_base + qi], qi - 1, body, 0)

    def finish_block(qi):
        o_ref[rows_of(qi, 1), :] = (acc_ref[...] / l_ref[...]).T.astype(BF16)

    build_q(0)
    tail_scores(0, False)
    start_block(0)
    build_q(1)
    tail_accumulate(0, False)
    tail_scores(1, True)
    finish_block(0)

    def query_block(qi, carry):
        start_block(qi)
        build_q(qi + 1)
        tail_accumulate(qi, True)
        tail_scores(qi + 1, True)
        finish_block(qi)
        return carry
    lax.fori_loop(1, nq - 1, query_block, 0)

    start_block(nq - 1)
    tail_accumulate(nq - 1, True)
    finish_block(nq - 1)


def _fox_attn(first_block, qkv, f_terms, *, batch, seq, heads, tq):
    n = qkv.shape[0]
    d = heads * FOX_HEAD_DIM
    nq = seq // tq
    f_cols = f_terms.shape[1]
    assert seq % tq == 0 and nq >= 2 and qkv.shape == (n, 3 * d) and first_block.shape == (batch * heads * nq,)
    blocks = 4 * seq * FOX_HEAD_DIM * 2 + seq * f_cols * 2
    temps = (seq + tq) * 2 * FOX_HEAD_DIM * 2 + tq * FOX_HEAD_DIM * 4 + 10 * tq * tq * 4
    return pl.pallas_call(
        functools.partial(_fox_attn_kernel, tq=tq, tk=tq),
        out_shape=jax.ShapeDtypeStruct((n, d), BF16),
        grid_spec=pltpu.PrefetchScalarGridSpec(
            num_scalar_prefetch=1,
            grid=(batch, heads),
            in_specs=[
                pl.BlockSpec((seq, FOX_HEAD_DIM), lambda b, h, first: (b, h)),
                pl.BlockSpec((seq, FOX_HEAD_DIM), lambda b, h, first: (b, heads + h)),
                pl.BlockSpec((seq, FOX_HEAD_DIM), lambda b, h, first: (b, 2 * heads + h)),
                pl.BlockSpec((seq, f_cols), lambda b, h, first: (b, 0)),
            ],
            out_specs=pl.BlockSpec((seq, FOX_HEAD_DIM), lambda b, h, first: (b, h)),
            scratch_shapes=[pltpu.VMEM((seq, 2 * FOX_HEAD_DIM), BF16), pltpu.VMEM((tq, 2 * FOX_HEAD_DIM), BF16),
                            pltpu.VMEM((2 * tq, tq), F32), pltpu.VMEM((1, tq), F32), pltpu.VMEM((1, tq), F32),
                            pltpu.VMEM((1, tq), F32), pltpu.VMEM((FOX_HEAD_DIM, tq), F32)]),
        compiler_params=pltpu.CompilerParams(
            dimension_semantics=("arbitrary", "arbitrary"),
            vmem_limit_bytes=_vmem_limit(blocks, temps)),
        name="fox_attn",
    )(first_block, qkv, qkv, qkv, f_terms)


def _fox_out_kernel(o_ref, w_ref, x_ref, y_ref):
    y_ref[...] = x_ref[...] + _dot(o_ref[...], w_ref[...].astype(BF16))


def _fox_out(o, w_out, mixer, x, *, tm, tn):
    n, d = x.shape
    assert n % tm == 0 and d % tn == 0
    blocks = tm * d * 2 + d * tn * 4 + 2 * tm * tn * 4
    temps = d * tn * 2 + tm * tn * 4
    return pl.pallas_call(
        _fox_out_kernel,
        out_shape=jax.ShapeDtypeStruct((n, d), F32),
        grid=(d // tn, n // tm),
        in_specs=[
            pl.BlockSpec((tm, d), lambda j, i: (i, 0)),
            pl.BlockSpec((None, d, tn), lambda j, i: (mixer, 0, j)),
            pl.BlockSpec((tm, tn), lambda j, i: (i, j)),
        ],
        out_specs=pl.BlockSpec((tm, tn), lambda j, i: (i, j)),
        compiler_params=pltpu.CompilerParams(
            dimension_semantics=("arbitrary", "arbitrary"),
            vmem_limit_bytes=_vmem_limit(blocks, temps)),
        name="fox_out",
    )(o, w_out, x)


def _fox(x, gains, layer, w_in_t, w_f_t, b_f, q_gains, k_gains, w_out, mixer, *, batch, seq, tm, tq):
    heads = x.shape[1] // FOX_HEAD_DIM
    qkv, f_terms, first = _fox_proj(x, gains, layer, w_in_t, w_f_t, b_f, q_gains, k_gains, mixer,
                                    seq=seq, tm=tm, tk=tq, tn=min(1024, x.shape[1]))
    first = first[:, :tm // tq, :heads].reshape(batch, seq // tq, heads).transpose(0, 2, 1).reshape(-1)
    o = _fox_attn(first, qkv, f_terms, batch=batch, seq=seq, heads=heads, tq=tq)
    return _fox_out(o, w_out, mixer, x, tm=tm, tn=min(1024, x.shape[1]))


def kernel(x, ffn1_norm, ffn1_w_gate, ffn1_w_up, ffn1_w_down, mix_norm, pool_w, pool_scale, fox_w_in, fox_b_f,
           fox_q_gain, fox_k_gain, fox_w_out, ffn2_norm, ffn2_w_gate, ffn2_w_up, ffn2_w_down):
    batch, seq, d = x.shape
    depth = ffn1_norm.shape[0]
    n_mixers = 2
    heads = d // FOX_HEAD_DIM
    tm = min(1024, seq)
    tq = min(512, seq)
    assert heads <= V7X_LANES

    w_in_t = jnp.swapaxes(fox_w_in, 1, 2)
    w_f_t = jnp.pad(w_in_t[:, 3 * d:, :], ((0, 0), (0, V7X_LANES - heads), (0, 0))).astype(BF16)
    b_f = jnp.pad(fox_b_f, ((0, 0), (0, V7X_LANES - heads))).reshape(-1, 1, V7X_LANES)

    h = x.reshape(batch * seq, d)
    for i in range(depth):
        h = _ffn(h, ffn1_norm, ffn1_w_gate, ffn1_w_up, ffn1_w_down, i, tm=tm)
        j = i // n_mixers
        if i % n_mixers == 0:
            h = _pool(h, mix_norm, i, pool_w, pool_scale, j, seq=seq, tm=min(512, seq))
        else:
            h = _fox(h, mix_norm, i, w_in_t, w_f_t, b_f, fox_q_gain, fox_k_gain, fox_w_out, j,
                     batch=batch, seq=seq, tm=tm, tq=tq)
        h = _ffn(h, ffn2_norm, ffn2_w_gate, ffn2_w_up, ffn2_w_down, i, tm=tm)
    return h.reshape(batch, seq, d)
```

```python
import functools
import math

import jax
import jax.numpy as jnp
from jax import lax
from jax.experimental import pallas as pl
from jax.experimental.pallas import tpu as pltpu

RMS_EPS = 1e-6
FFN_RESIDUAL_WEIGHT = 0.5
POOL_WINDOWS = (2, 4, 8, 16)
FOX_HEAD_DIM = 128
NEG_LARGE = -1e30
LOG2E = math.log2(math.e)

SKIP_LOG2_MARGIN = 160.0
QK_BOUND_SLACK = 1.01

V7X_LANES = 128
V7X_SUBLANES = 8
V7X_VMEM_BYTES = 64 * 1024 * 1024
POOL_HALO = 32
POOL_FIRST_ROW = 8
F_TERMS = 3

BF16 = jnp.bfloat16
F32 = jnp.float32


def _vmem_limit(block_bytes, temp_bytes):
    return min(2 * block_bytes + temp_bytes + (4 << 20), V7X_VMEM_BYTES - (2 << 20))


def _rms_norm(x, gain):
    ms = jnp.mean(x * x, axis=-1, keepdims=True)
    return x * lax.rsqrt(ms + RMS_EPS) * gain


def _dot(a, b):
    return jnp.dot(a, b, preferred_element_type=F32)


def _dot_nt(a, b):
    return lax.dot_general(a, b, (((1,), (1,)), ((), ())), preferred_element_type=F32)


def _split_bf16(x):
    terms = []
    for _ in range(F_TERMS):
        t = x.astype(BF16)
        terms.append(t)
        x = x - t.astype(F32)
    return terms


def _ffn_kernel(x_hbm, g_ref, wg_hbm, wu_hbm, wd_hbm, o_ref, x_buf, h_ref, wg_buf, wu_buf, wd_buf, w_sems, x_sem,
                *, layer, first_step_chunk):
    i = pl.program_id(0)
    j = pl.program_id(1)
    n_i = pl.num_programs(0)
    n_j = pl.num_programs(1)
    last = n_j - 1
    tm = x_buf.shape[0]
    slots, _, tf = wg_buf.shape
    per_step = slots // 2
    step = i * n_j + j
    n_steps = n_i * n_j

    def tile_copies(t):
        col = pl.multiple_of(lax.rem(t, n_j * per_step) * tf, tf)
        slot = lax.rem(t, slots)
        return (pltpu.make_async_copy(wg_hbm.at[layer, :, pl.ds(col, tf)], wg_buf.at[slot], w_sems.at[0, slot]),
                pltpu.make_async_copy(wu_hbm.at[layer, :, pl.ds(col, tf)], wu_buf.at[slot], w_sems.at[1, slot]),
                pltpu.make_async_copy(wd_hbm.at[layer, pl.ds(col, tf), :], wd_buf.at[slot], w_sems.at[2, slot]))

    def x_copy(tile):
        return pltpu.make_async_copy(x_hbm.at[pl.ds(pl.multiple_of(tile * tm, tm), tm), :], x_buf, x_sem)

    @pl.when(step == 0)
    def _():
        x_copy(0).start()
        for t in range(per_step):
            for copy in tile_copies(jnp.int32(t)):
                copy.start()

    @pl.when(step + 1 < n_steps)
    def _():
        for k in range(per_step):
            for copy in tile_copies((step + 1) * per_step + k):
                copy.start()

    @pl.when((j == 1) & (i + 1 < n_i))
    def _():
        x_copy(i + 1).start()

    @pl.when(j == 0)
    def _():
        x_copy(i).wait()

    for k in range(per_step):
        for copy in tile_copies(step * per_step + k):
            copy.wait()

    def weights():
        tiles = []
        for k in range(per_step):
            slot = lax.rem(step * per_step + k, slots)
            tiles.append((wg_buf[slot].astype(BF16), wu_buf[slot].astype(BF16), wd_buf[slot].astype(BF16)))
        return tiles

    def activations(h, tiles):
        acts = []
        for wg, wu, _ in tiles:
            gate = _dot(h, wg)
            up = _dot(h, wu)
            acts.append((gate * jax.nn.sigmoid(gate) * up).astype(BF16))
        return acts

    def down(acts, tiles):
        out = None
        for act, (_, _, wd) in zip(acts, tiles):
            term = _dot(act, wd)
            out = term if out is None else out + term
        return out

    def partial_out(h, tiles):
        return down(activations(h, tiles), tiles)

    @pl.when(j == 0)
    def _():
        tiles = weights()
        pending = None
        for r0 in range(0, tm, first_step_chunk):
            rows = slice(r0, r0 + first_step_chunk)
            h = _rms_norm(x_buf[rows, :], g_ref[...]).astype(BF16)
            h_ref[rows, :] = h
            acts = activations(h, tiles)
            if pending is not None:
                o_ref[pending[0], :] = x_buf[pending[0], :] * (1.0 / FFN_RESIDUAL_WEIGHT) + down(pending[1], tiles)
            pending = (rows, acts)
        o_ref[pending[0], :] = x_buf[pending[0], :] * (1.0 / FFN_RESIDUAL_WEIGHT) + down(pending[1], tiles)

    @pl.when((j > 0) & (j < last))
    def _():
        o_ref[...] += partial_out(h_ref[...], weights())

    @pl.when(j == last)
    def _():
        o_ref[...] = FFN_RESIDUAL_WEIGHT * (o_ref[...] + partial_out(h_ref[...], weights()))


def _ffn(x, gains, w_gate, w_up, w_down, layer, *, tm, tf=256, tiles_per_step=2):
    n, d = x.shape
    f = w_gate.shape[2]
    n_j = f // (tf * tiles_per_step)
    slots = 2 * tiles_per_step
    assert n % tm == 0 and f % (tf * tiles_per_step) == 0 and n_j >= 2
    assert math.frexp(FFN_RESIDUAL_WEIGHT)[0] == 0.5, "residual weight must be a power of two"
    blocks = tm * d * 4 + d * 4
    temps = (tm * d * 4 + tm * d * 2 + slots * 3 * d * tf * 4
             + tiles_per_step * (3 * d * tf * 2 + 3 * tm * tf * 4))
    return pl.pallas_call(
        functools.partial(_ffn_kernel, layer=layer, first_step_chunk=min(256, tm)),
        out_shape=jax.ShapeDtypeStruct((n, d), F32),
        grid=(n // tm, n_j),
        in_specs=[
            pl.BlockSpec(memory_space=pl.ANY),
            pl.BlockSpec((None, 1, d), lambda i, j: (layer, 0, 0)),
            pl.BlockSpec(memory_space=pl.ANY),
            pl.BlockSpec(memory_space=pl.ANY),
            pl.BlockSpec(memory_space=pl.ANY),
        ],
        out_specs=pl.BlockSpec((tm, d), lambda i, j: (i, 0)),
        scratch_shapes=[pltpu.VMEM((tm, d), F32), pltpu.VMEM((tm, d), BF16),
                        pltpu.VMEM((slots, d, tf), F32), pltpu.VMEM((slots, d, tf), F32),
                        pltpu.VMEM((slots, tf, d), F32), pltpu.SemaphoreType.DMA((3, slots)),
                        pltpu.SemaphoreType.DMA(())],
        compiler_params=pltpu.CompilerParams(
            dimension_semantics=("arbitrary", "arbitrary"),
            vmem_limit_bytes=_vmem_limit(blocks, temps)),
        name="ffn",
    )(x, gains.reshape(gains.shape[0], 1, d), w_gate, w_up, w_down)


def _pool_kernel(x_ref, xh_ref, g_ref, w_ref, s_ref, o_ref, h_ref, p_ref, q_ref, *, tiles_per_batch):
    tm, d = x_ref.shape
    group = d // len(POOL_WINDOWS)
    rows = tm + POOL_HALO
    tile_in_batch = lax.rem(pl.program_id(0), tiles_per_batch)

    gain = g_ref[...]
    halo = _rms_norm(xh_ref[...], gain)
    h_ref[0:POOL_HALO, :] = jnp.where(tile_in_batch == 0, 0.0, halo)
    h_ref[POOL_HALO:, :] = _rms_norm(x_ref[...], gain)

    p_ref[0:POOL_FIRST_ROW, :] = jnp.zeros((POOL_FIRST_ROW, d), F32)
    q_ref[0:POOL_FIRST_ROW, :] = jnp.zeros((POOL_FIRST_ROW, d), F32)
    src = h_ref
    first_exact_row = POOL_FIRST_ROW
    for step in range(len(POOL_WINDOWS)):
        shift = 1 << step
        assert POOL_WINDOWS[step] == 2 * shift and shift <= POOL_FIRST_ROW
        first_exact_row += shift if step else 0
        dst = p_ref if step % 2 == 0 else q_ref
        c0 = step * group
        dst[POOL_FIRST_ROW:, c0:] = (src[POOL_FIRST_ROW:, c0:]
                                     + src[POOL_FIRST_ROW - shift:rows - shift, c0:])
        src = dst
    assert first_exact_row <= POOL_HALO

    pos = tile_in_batch * tm + lax.broadcasted_iota(jnp.int32, (tm, 1), 0)
    for gi, window in enumerate(POOL_WINDOWS):
        cols = slice(gi * group, (gi + 1) * group)
        sums = (p_ref if gi % 2 == 0 else q_ref)[POOL_HALO:, cols]
        inv_count = 1.0 / jnp.minimum(pos + 1, window).astype(F32)
        diff = (sums * inv_count - h_ref[POOL_HALO:, cols]).astype(BF16)
        y = _dot(diff, w_ref[gi].astype(BF16))
        o_ref[:, cols] = x_ref[:, cols] + y * s_ref[:, cols]


def _pool(x, gains, layer, w_groups, scales, mixer, *, seq, tm=512):
    n, d = x.shape
    _, groups, gsz, _ = w_groups.shape
    assert seq % tm == 0 and tm % POOL_HALO == 0 and groups * gsz == d
    halo_blocks = tm // POOL_HALO
    blocks = 2 * tm * d * 4 + POOL_HALO * d * 4 + groups * gsz * gsz * 4 + 2 * d * 4
    temps = 3 * (tm + POOL_HALO) * d * 4 + 2 * tm * d * 4
    return pl.pallas_call(
        functools.partial(_pool_kernel, tiles_per_batch=seq // tm),
        out_shape=jax.ShapeDtypeStruct((n, d), F32),
        grid=(n // tm,),
        in_specs=[
            pl.BlockSpec((tm, d), lambda i: (i, 0)),
            pl.BlockSpec((POOL_HALO, d), lambda i: (jnp.maximum(i * halo_blocks - 1, 0), 0)),
            pl.BlockSpec((None, 1, d), lambda i: (layer, 0, 0)),
            pl.BlockSpec((None, groups, gsz, gsz), lambda i: (mixer, 0, 0, 0)),
            pl.BlockSpec((None, 1, d), lambda i: (mixer, 0, 0)),
        ],
        out_specs=pl.BlockSpec((tm, d), lambda i: (i, 0)),
        scratch_shapes=[pltpu.VMEM((tm + POOL_HALO, d), F32)] * 3,
        compiler_params=pltpu.CompilerParams(
            dimension_semantics=("arbitrary",),
            vmem_limit_bytes=_vmem_limit(blocks, temps)),
        name="pool",
    )(x, x, gains.reshape(gains.shape[0], 1, d), w_groups, scales.reshape(scales.shape[0], 1, d))


def _log_sigmoid(z):
    return jnp.minimum(z, 0.0) - jnp.log1p(jnp.exp(-jnp.abs(z)))


def _fox_proj_kernel(x_ref, g_ref, w_ref, wf_ref, bf_ref, gq_ref, gk_ref, qkv_ref, f_ref, ks_ref,
                     h_ref, carry_ref, fmin_ref, *, tiles_per_batch, qk_tiles, tk, chunk):
    i = pl.program_id(0)
    j = pl.program_id(1)
    tm, tn = qkv_ref.shape
    blocks_per_tile = tm // tk

    def project_qk(is_q, rows, w):
        acc = _dot_nt(h_ref[rows, :], w)
        gain = jnp.where(is_q, gq_ref[...], gk_ref[...])
        post = jnp.where(is_q, LOG2E / math.sqrt(FOX_HEAD_DIM), 1.0)
        for c0 in range(0, tn, FOX_HEAD_DIM):
            y = _rms_norm(acc[:, c0:c0 + FOX_HEAD_DIM], gain) * post
            qkv_ref[rows, c0:c0 + FOX_HEAD_DIM] = y.astype(BF16)

    @pl.when(j == 0)
    def _():
        tile_in_batch = lax.rem(i, tiles_per_batch)

        @pl.when(tile_in_batch == 0)
        def _():
            carry_ref[...] = jnp.zeros_like(carry_ref)
            fmin_ref[...] = jnp.zeros_like(fmin_ref)

        for c0 in range(0, tm, chunk):
            h_ref[c0:c0 + chunk, :] = _rms_norm(x_ref[c0:c0 + chunk, :], g_ref[...]).astype(BF16)
        log_f_terms = _split_bf16(_log_sigmoid(_dot_nt(h_ref[...], wf_ref[...]) + bf_ref[...]))
        project_qk(True, slice(None), w_ref[...].astype(BF16))

        r = lax.broadcasted_iota(jnp.int32, (chunk, chunk), 0)
        c = lax.broadcasted_iota(jnp.int32, (chunk, chunk), 1)
        tri = (c <= r).astype(BF16)
        carry = carry_ref[...]
        block_max = [None] * blocks_per_tile
        block_min = [None] * blocks_per_tile
        for c0 in range(0, tm, chunk):
            prefix = sum(_dot(tri, t[c0:c0 + chunk, :]) for t in log_f_terms)
            total = prefix + carry
            carry = total[chunk - 1:chunk, :]
            f2 = total * LOG2E
            for t, term in enumerate(_split_bf16(f2)):
                f_ref[c0:c0 + chunk, t * V7X_LANES:(t + 1) * V7X_LANES] = term
            blk = c0 // tk
            hi = jnp.max(f2, axis=0, keepdims=True)
            lo = jnp.min(f2, axis=0, keepdims=True)
            block_max[blk] = hi if block_max[blk] is None else jnp.maximum(block_max[blk], hi)
            block_min[blk] = lo if block_min[blk] is None else jnp.minimum(block_min[blk], lo)
        carry_ref[...] = carry

        qk_bound = (LOG2E * math.sqrt(FOX_HEAD_DIM) * QK_BOUND_SLACK
                    * jnp.max(jnp.abs(gq_ref[...]), axis=-1, keepdims=True)
                    * jnp.max(jnp.abs(gk_ref[...]), axis=-1, keepdims=True))
        n_blocks = fmin_ref.shape[0]
        row = lax.broadcasted_iota(jnp.int32, fmin_ref.shape, 0)
        ks_ref[...] = jnp.zeros_like(ks_ref)
        for blk in range(blocks_per_tile):
            qb = tile_in_batch * blocks_per_tile + blk
            fmin_ref[pl.ds(qb, 1), :] = block_min[blk]
            gap = 2.0 * qk_bound + block_max[blk] - fmin_ref[...]
            needed = (gap > -SKIP_LOG2_MARGIN) | (row >= qb)
            ks_ref[blk:blk + 1, :] = jnp.min(jnp.where(needed, row, n_blocks), axis=0, keepdims=True)

    @pl.when((j > 0) & (j < qk_tiles))
    def _():
        w = w_ref[...].astype(BF16)
        for r0 in range(0, tm, tm // 2):
            project_qk(j < qk_tiles // 2, slice(r0, r0 + tm // 2), w)

    @pl.when(j >= qk_tiles)
    def _():
        qkv_ref[...] = _dot_nt(h_ref[...], w_ref[...].astype(BF16)).astype(BF16)


def _fox_proj(x, gains, layer, w_in_t, w_f_t, b_f, q_gains, k_gains, mixer, *, seq, tm, tk, tn, chunk=256):
    n, d = x.shape
    assert seq % tm == 0 and tm % tk == 0 and tk % chunk == 0 and d % tn == 0 and tn % FOX_HEAD_DIM == 0
    assert tm // tk <= V7X_SUBLANES
    qk_tiles = 2 * d // tn
    f_cols = F_TERMS * V7X_LANES
    blocks = tm * d * 4 + d * tn * 4 + d * V7X_LANES * 2 + tm * tn * 2 + tm * f_cols * 2
    temps = tm * d * 2 + d * tn * 2 + 3 * tm * tn * 4 + 4 * chunk * chunk * 4
    return pl.pallas_call(
        functools.partial(_fox_proj_kernel, tiles_per_batch=seq // tm, qk_tiles=qk_tiles, tk=tk, chunk=chunk),
        out_shape=(jax.ShapeDtypeStruct((n, 3 * d), BF16),
                   jax.ShapeDtypeStruct((n, f_cols), BF16),
                   jax.ShapeDtypeStruct((n // tm, V7X_SUBLANES, V7X_LANES), jnp.int32)),
        grid=(n // tm, 3 * d // tn),
        in_specs=[
            pl.BlockSpec((tm, d), lambda i, j: (i, 0)),
            pl.BlockSpec((None, 1, d), lambda i, j: (layer, 0, 0)),
            pl.BlockSpec((None, tn, d), lambda i, j: (mixer, j, 0)),
            pl.BlockSpec((None, V7X_LANES, d), lambda i, j: (mixer, 0, 0)),
            pl.BlockSpec((None, 1, V7X_LANES), lambda i, j: (mixer, 0, 0)),
            pl.BlockSpec((None, 1, FOX_HEAD_DIM), lambda i, j: (mixer, 0, 0)),
            pl.BlockSpec((None, 1, FOX_HEAD_DIM), lambda i, j: (mixer, 0, 0)),
        ],
        out_specs=(pl.BlockSpec((tm, tn), lambda i, j: (i, j)),
                   pl.BlockSpec((tm, f_cols), lambda i, j: (i, 0)),
                   pl.BlockSpec((None, V7X_SUBLANES, V7X_LANES), lambda i, j: (i, 0, 0))),
        scratch_shapes=[pltpu.VMEM((tm, d), BF16), pltpu.VMEM((1, V7X_LANES), F32),
                        pltpu.VMEM((seq // tk, V7X_LANES), F32)],
        compiler_params=pltpu.CompilerParams(
            dimension_semantics=("arbitrary", "arbitrary"),
            vmem_limit_bytes=_vmem_limit(blocks, temps)),
        name="fox_proj",
    )(x, gains.reshape(gains.shape[0], 1, d), w_in_t, w_f_t, b_f,
      q_gains.reshape(q_gains.shape[0], 1, -1), k_gains.reshape(k_gains.shape[0], 1, -1))


def _fox_attn_kernel(first_ref, q_ref, k_ref, v_ref, f_ref, o_ref, kaug_ref, qaug_ref, s_ref, cmax_ref, m_ref,
                     l_ref, acc_ref, *, tq, tk):
    head = pl.program_id(1)
    seq = k_ref.shape[0]
    nq = seq // tq
    assert tq == tk
    table_base = (pl.program_id(0) * pl.num_programs(1) + head) * nq

    r = lax.broadcasted_iota(jnp.int32, (F_TERMS * V7X_LANES, V7X_LANES), 0)
    c = lax.broadcasted_iota(jnp.int32, (F_TERMS * V7X_LANES, V7X_LANES), 1)
    sel_q = (r == c * V7X_LANES + head).astype(BF16)
    sel_k = -(r == (c - F_TERMS) * V7X_LANES + head).astype(BF16)
    lane = lax.broadcasted_iota(jnp.int32, (1, V7X_LANES), 1)
    ones_q = ((lane >= F_TERMS) & (lane < 2 * F_TERMS)).astype(F32)
    ones_k = (lane < F_TERMS).astype(F32)

    kaug_ref[:, 0:FOX_HEAD_DIM] = k_ref[...]
    kaug_ref[:, FOX_HEAD_DIM:] = (_dot(f_ref[...], sel_k) + ones_k).astype(BF16)

    def rows_of(block, n_blocks):
        return pl.ds(pl.multiple_of(block * tk, tk), n_blocks * tk)

    def build_q(qi):
        qaug_ref[:, 0:FOX_HEAD_DIM] = q_ref[rows_of(qi, 1), :]
        qaug_ref[:, FOX_HEAD_DIM:] = (_dot(f_ref[rows_of(qi, 1), :], sel_q) + ones_q).astype(BF16)

    def logits(rows):
        return _dot_nt(kaug_ref[rows, :], qaug_ref[...])

    def tail_scores(qi, with_prev):
        s = logits(rows_of(qi - 1, 2) if with_prev else rows_of(qi, 1))
        key = lax.broadcasted_iota(jnp.int32, (tk, tq), 0)
        qry = lax.broadcasted_iota(jnp.int32, (tk, tq), 1)
        diag = jnp.where(key <= qry, s[-tk:], NEG_LARGE)
        cmax = jnp.max(diag, axis=0, keepdims=True)
        if with_prev:
            cmax = jnp.maximum(cmax, jnp.max(s[:tk], axis=0, keepdims=True))
            s_ref[0:tk, :] = s[:tk]
            s_ref[tk:, :] = diag
        else:
            s_ref[0:tk, :] = diag
        cmax_ref[...] = cmax

    def online_update(s, block_max, v_rows):
        m_old = m_ref[...]
        m_new = jnp.maximum(m_old, block_max)
        alpha = jnp.exp2(m_old - m_new)
        p = jnp.exp2(s - m_new)
        l_ref[...] = alpha * l_ref[...] + jnp.sum(p, axis=0, keepdims=True)
        pv = lax.dot_general(v_ref[v_rows, :], p.astype(BF16), (((0,), (0,)), ((), ())),
                             preferred_element_type=F32)
        acc_ref[...] = alpha * acc_ref[...] + pv
        m_ref[...] = m_new

    def tail_accumulate(qi, with_prev):
        if with_prev:
            online_update(s_ref[...], cmax_ref[...], rows_of(qi - 1, 2))
        else:
            online_update(s_ref[0:tk, :], cmax_ref[...], rows_of(qi, 1))

    def start_block(qi):
        m_ref[...] = jnp.full_like(m_ref, NEG_LARGE)
        l_ref[...] = jnp.zeros_like(l_ref)
        acc_ref[...] = jnp.zeros_like(acc_ref)

        def body(kb, c):
            s = logits(rows_of(kb, 1))
            online_update(s, jnp.max(s, axis=0, keepdims=True), rows_of(kb, 1))
            return c
        lax.fori_loop(first_ref[table_base + qi], qi - 1, body, 0)

    def finish_block(qi):
        o_ref[rows_of(qi, 1), :] = (acc_ref[...] / l_ref[...]).T.astype(BF16)

    build_q(0)
    tail_scores(0, False)
    start_block(0)
    build_q(1)
    tail_accumulate(0, False)
    tail_scores(1, True)
    finish_block(0)

    def query_block(qi, carry):
        start_block(qi)
        build_q(qi + 1)
        tail_accumulate(qi, True)
        tail_scores(qi + 1, True)
        finish_block(qi)
        return carry
    lax.fori_loop(1, nq - 1, query_block, 0)

    start_block(nq - 1)
    tail_accumulate(nq - 1, True)
    finish_block(nq - 1)


def _fox_attn(first_block, qkv, f_terms, *, batch, seq, heads, tq):
    n = qkv.shape[0]
    d = heads * FOX_HEAD_DIM
    nq = seq // tq
    f_cols = f_terms.shape[1]
    assert seq % tq == 0 and nq >= 2 and qkv.shape == (n, 3 * d) and first_block.shape == (batch * heads * nq,)
    blocks = 4 * seq * FOX_HEAD_DIM * 2 + seq * f_cols * 2
    temps = (seq + tq) * 2 * FOX_HEAD_DIM * 2 + tq * FOX_HEAD_DIM * 4 + 10 * tq * tq * 4
    return pl.pallas_call(
        functools.partial(_fox_attn_kernel, tq=tq, tk=tq),
        out_shape=jax.ShapeDtypeStruct((n, d), BF16),
        grid_spec=pltpu.PrefetchScalarGridSpec(
            num_scalar_prefetch=1,
            grid=(batch, heads),
            in_specs=[
                pl.BlockSpec((seq, FOX_HEAD_DIM), lambda b, h, first: (b, h)),
                pl.BlockSpec((seq, FOX_HEAD_DIM), lambda b, h, first: (b, heads + h)),
                pl.BlockSpec((seq, FOX_HEAD_DIM), lambda b, h, first: (b, 2 * heads + h)),
                pl.BlockSpec((seq, f_cols), lambda b, h, first: (b, 0)),
            ],
            out_specs=pl.BlockSpec((seq, FOX_HEAD_DIM), lambda b, h, first: (b, h)),
            scratch_shapes=[pltpu.VMEM((seq, 2 * FOX_HEAD_DIM), BF16), pltpu.VMEM((tq, 2 * FOX_HEAD_DIM), BF16),
                            pltpu.VMEM((2 * tq, tq), F32), pltpu.VMEM((1, tq), F32), pltpu.VMEM((1, tq), F32),
                            pltpu.VMEM((1, tq), F32), pltpu.VMEM((FOX_HEAD_DIM, tq), F32)]),
        compiler_params=pltpu.CompilerParams(
            dimension_semantics=("arbitrary", "arbitrary"),
            vmem_limit_bytes=_vmem_limit(blocks, temps)),
        name="fox_attn",
    )(first_block, qkv, qkv, qkv, f_terms)


def _fox_out_kernel(o_ref, w_ref, x_ref, y_ref):
    y_ref[...] = x_ref[...] + _dot(o_ref[...], w_ref[...].astype(BF16))


def _fox_out(o, w_out, mixer, x, *, tm, tn):
    n, d = x.shape
    assert n % tm == 0 and d % tn == 0
    blocks = tm * d * 2 + d * tn * 4 + 2 * tm * tn * 4
    temps = d * tn * 2 + tm * tn * 4
    return pl.pallas_call(
        _fox_out_kernel,
        out_shape=jax.ShapeDtypeStruct((n, d), F32),
        grid=(d // tn, n // tm),
        in_specs=[
            pl.BlockSpec((tm, d), lambda j, i: (i, 0)),
            pl.BlockSpec((None, d, tn), lambda j, i: (mixer, 0, j)),
            pl.BlockSpec((tm, tn), lambda j, i: (i, j)),
        ],
        out_specs=pl.BlockSpec((tm, tn), lambda j, i: (i, j)),
        compiler_params=pltpu.CompilerParams(
            dimension_semantics=("arbitrary", "arbitrary"),
            vmem_limit_bytes=_vmem_limit(blocks, temps)),
        name="fox_out",
    )(o, w_out, x)


def _fox(x, gains, layer, w_in_t, w_f_t, b_f, q_gains, k_gains, w_out, mixer, *, batch, seq, tm, tq):
    heads = x.shape[1] // FOX_HEAD_DIM
    qkv, f_terms, first = _fox_proj(x, gains, layer, w_in_t, w_f_t, b_f, q_gains, k_gains, mixer,
                                    seq=seq, tm=tm, tk=tq, tn=min(1024, x.shape[1]))
    first = first[:, :tm // tq, :heads].reshape(batch, seq // tq, heads).transpose(0, 2, 1).reshape(-1)
    o = _fox_attn(first, qkv, f_terms, batch=batch, seq=seq, heads=heads, tq=tq)
    return _fox_out(o, w_out, mixer, x, tm=tm, tn=min(1024, x.shape[1]))


def kernel(x, ffn1_norm, ffn1_w_gate, ffn1_w_up, ffn1_w_down, mix_norm, pool_w, pool_scale, fox_w_in, fox_b_f,
           fox_q_gain, fox_k_gain, fox_w_out, ffn2_norm, ffn2_w_gate, ffn2_w_up, ffn2_w_down):
    batch, seq, d = x.shape
    depth = ffn1_norm.shape[0]
    n_mixers = 2
    heads = d // FOX_HEAD_DIM
    tm = min(1024, seq)
    tq = min(512, seq)
    assert heads <= V7X_LANES

    w_in_t = jnp.swapaxes(fox_w_in, 1, 2)
    w_f_t = jnp.pad(w_in_t[:, 3 * d:, :], ((0, 0), (0, V7X_LANES - heads), (0, 0))).astype(BF16)
    b_f = jnp.pad(fox_b_f, ((0, 0), (0, V7X_LANES - heads))).reshape(-1, 1, V7X_LANES)

    h = x.reshape(batch * seq, d)
    for i in range(depth):
        h = _ffn(h, ffn1_norm, ffn1_w_gate, ffn1_w_up, ffn1_w_down, i, tm=tm)
        j = i // n_mixers
        if i % n_mixers == 0:
            h = _pool(h, mix_norm, i, pool_w, pool_scale, j, seq=seq, tm=min(512, seq))
        else:
            h = _fox(h, mix_norm, i, w_in_t, w_f_t, b_f, fox_q_gain, fox_k_gain, fox_w_out, j,
                     batch=batch, seq=seq, tm=tm, tq=tq)
        h = _ffn(h, ffn2_norm, ffn2_w_gate, ffn2_w_up, ffn2_w_down, i, tm=tm)
    return h.reshape(batch, seq, d)
```

```python
import functools
import math

import jax
import jax.numpy as jnp
from jax import lax
from jax.experimental import pallas as pl
from jax.experimental.pallas import tpu as pltpu

RMS_EPS = 1e-6
FFN_RESIDUAL_WEIGHT = 0.5
POOL_WINDOWS = (2, 4, 8, 16)
FOX_HEAD_DIM = 128
NEG_LARGE = -1e30
LOG2E = math.log2(math.e)

SKIP_LOG2_MARGIN = 160.0
QK_BOUND_SLACK = 1.01

V7X_LANES = 128
V7X_SUBLANES = 8
V7X_VMEM_BYTES = 64 * 1024 * 1024
POOL_HALO = 32
POOL_FIRST_ROW = 8
F_TERMS = 3

BF16 = jnp.bfloat16
F32 = jnp.float32


def _vmem_limit(block_bytes, temp_bytes):
    return min(2 * block_bytes + temp_bytes + (4 << 20), V7X_VMEM_BYTES - (2 << 20))


def _rms_norm(x, gain):
    ms = jnp.mean(x * x, axis=-1, keepdims=True)
    return x * lax.rsqrt(ms + RMS_EPS) * gain


def _dot(a, b):
    return jnp.dot(a, b, preferred_element_type=F32)


def _dot_nt(a, b):
    return lax.dot_general(a, b, (((1,), (1,)), ((), ())), preferred_element_type=F32)


def _split_bf16(x):
    terms = []
    for _ in range(F_TERMS):
        t = x.astype(BF16)
        terms.append(t)
        x = x - t.astype(F32)
    return terms


def _ffn_kernel(x_hbm, g_ref, wg_hbm, wu_hbm, wd_hbm, o_ref, x_buf, h_ref, wg_buf, wu_buf, wd_buf, w_sems, x_sem,
                *, layer, first_step_chunk):
    i = pl.program_id(0)
    j = pl.program_id(1)
    n_i = pl.num_programs(0)
    n_j = pl.num_programs(1)
    last = n_j - 1
    tm = x_buf.shape[0]
    slots, _, tf = wg_buf.shape
    per_step = slots // 2
    step = i * n_j + j
    n_steps = n_i * n_j

    def tile_copies(t):
        col = pl.multiple_of(lax.rem(t, n_j * per_step) * tf, tf)
        slot = lax.rem(t, slots)
        return (pltpu.make_async_copy(wg_hbm.at[layer, :, pl.ds(col, tf)], wg_buf.at[slot], w_sems.at[0, slot]),
                pltpu.make_async_copy(wu_hbm.at[layer, :, pl.ds(col, tf)], wu_buf.at[slot], w_sems.at[1, slot]),
                pltpu.make_async_copy(wd_hbm.at[layer, pl.ds(col, tf), :], wd_buf.at[slot], w_sems.at[2, slot]))

    def x_copy(tile):
        return pltpu.make_async_copy(x_hbm.at[pl.ds(pl.multiple_of(tile * tm, tm), tm), :], x_buf, x_sem)

    @pl.when(step == 0)
    def _():
        x_copy(0).start()
        for t in range(per_step):
            for copy in tile_copies(jnp.int32(t)):
                copy.start()

    @pl.when(step + 1 < n_steps)
    def _():
        for k in range(per_step):
            for copy in tile_copies((step + 1) * per_step + k):
                copy.start()

    @pl.when((j == 1) & (i + 1 < n_i))
    def _():
        x_copy(i + 1).start()

    @pl.when(j == 0)
    def _():
        x_copy(i).wait()

    for k in range(per_step):
        for copy in tile_copies(step * per_step + k):
            copy.wait()

    def weights():
        tiles = []
        for k in range(per_step):
            slot = lax.rem(step * per_step + k, slots)
            tiles.append((wg_buf[slot].astype(BF16), wu_buf[slot].astype(BF16), wd_buf[slot].astype(BF16)))
        return tiles

    def activations(h, tiles):
        acts = []
        for wg, wu, _ in tiles:
            gate = _dot(h, wg)
            up = _dot(h, wu)
            acts.append((gate * jax.nn.sigmoid(gate) * up).astype(BF16))
        return acts

    def down(acts, tiles):
        out = None
        for act, (_, _, wd) in zip(acts, tiles):
            term = _dot(act, wd)
            out = term if out is None else out + term
        return out

    def partial_out(h, tiles):
        return down(activations(h, tiles), tiles)

    @pl.when(j == 0)
    def _():
        tiles = weights()
        pending = None
        for r0 in range(0, tm, first_step_chunk):
            rows = slice(r0, r0 + first_step_chunk)
            h = _rms_norm(x_buf[rows, :], g_ref[...]).astype(BF16)
            h_ref[rows, :] = h
            acts = activations(h, tiles)
            if pending is not None:
                o_ref[pending[0], :] = x_buf[pending[0], :] * (1.0 / FFN_RESIDUAL_WEIGHT) + down(pending[1], tiles)
            pending = (rows, acts)
        o_ref[pending[0], :] = x_buf[pending[0], :] * (1.0 / FFN_RESIDUAL_WEIGHT) + down(pending[1], tiles)

    @pl.when((j > 0) & (j < last))
    def _():
        o_ref[...] += partial_out(h_ref[...], weights())

    @pl.when(j == last)
    def _():
        o_ref[...] = FFN_RESIDUAL_WEIGHT * (o_ref[...] + partial_out(h_ref[...], weights()))


def _ffn(x, gains, w_gate, w_up, w_down, layer, *, tm, tf=512, tiles_per_step=1):
    n, d = x.shape
    f = w_gate.shape[2]
    n_j = f // (tf * tiles_per_step)
    slots = 2 * tiles_per_step
    assert n % tm == 0 and f % (tf * tiles_per_step) == 0 and n_j >= 2
    assert math.frexp(FFN_RESIDUAL_WEIGHT)[0] == 0.5, "residual weight must be a power of two"
    blocks = tm * d * 4 + d * 4
    temps = (tm * d * 4 + tm * d * 2 + slots * 3 * d * tf * 4
             + tiles_per_step * (3 * d * tf * 2 + 3 * tm * tf * 4))
    return pl.pallas_call(
        functools.partial(_ffn_kernel, layer=layer, first_step_chunk=min(256, tm)),
        out_shape=jax.ShapeDtypeStruct((n, d), F32),
        grid=(n // tm, n_j),
        in_specs=[
            pl.BlockSpec(memory_space=pl.ANY),
            pl.BlockSpec((None, 1, d), lambda i, j: (layer, 0, 0)),
            pl.BlockSpec(memory_space=pl.ANY),
            pl.BlockSpec(memory_space=pl.ANY),
            pl.BlockSpec(memory_space=pl.ANY),
        ],
        out_specs=pl.BlockSpec((tm, d), lambda i, j: (i, 0)),
        scratch_shapes=[pltpu.VMEM((tm, d), F32), pltpu.VMEM((tm, d), BF16),
                        pltpu.VMEM((slots, d, tf), F32), pltpu.VMEM((slots, d, tf), F32),
                        pltpu.VMEM((slots, tf, d), F32), pltpu.SemaphoreType.DMA((3, slots)),
                        pltpu.SemaphoreType.DMA(())],
        compiler_params=pltpu.CompilerParams(
            dimension_semantics=("arbitrary", "arbitrary"),
            vmem_limit_bytes=_vmem_limit(blocks, temps)),
        name="ffn",
    )(x, gains.reshape(gains.shape[0], 1, d), w_gate, w_up, w_down)


def _pool_kernel(x_ref, xh_ref, g_ref, w_ref, s_ref, o_ref, h_ref, p_ref, q_ref, *, tiles_per_batch):
    tm, d = x_ref.shape
    group = d // len(POOL_WINDOWS)
    rows = tm + POOL_HALO
    tile_in_batch = lax.rem(pl.program_id(0), tiles_per_batch)

    gain = g_ref[...]
    halo = _rms_norm(xh_ref[...], gain)
    h_ref[0:POOL_HALO, :] = jnp.where(tile_in_batch == 0, 0.0, halo)
    h_ref[POOL_HALO:, :] = _rms_norm(x_ref[...], gain)

    p_ref[0:POOL_FIRST_ROW, :] = jnp.zeros((POOL_FIRST_ROW, d), F32)
    q_ref[0:POOL_FIRST_ROW, :] = jnp.zeros((POOL_FIRST_ROW, d), F32)
    src = h_ref
    first_exact_row = POOL_FIRST_ROW
    for step in range(len(POOL_WINDOWS)):
        shift = 1 << step
        assert POOL_WINDOWS[step] == 2 * shift and shift <= POOL_FIRST_ROW
        first_exact_row += shift if step else 0
        dst = p_ref if step % 2 == 0 else q_ref
        c0 = step * group
        dst[POOL_FIRST_ROW:, c0:] = (src[POOL_FIRST_ROW:, c0:]
                                     + src[POOL_FIRST_ROW - shift:rows - shift, c0:])
        src = dst
    assert first_exact_row <= POOL_HALO

    pos = tile_in_batch * tm + lax.broadcasted_iota(jnp.int32, (tm, 1), 0)
    for gi, window in enumerate(POOL_WINDOWS):
        cols = slice(gi * group, (gi + 1) * group)
        sums = (p_ref if gi % 2 == 0 else q_ref)[POOL_HALO:, cols]
        inv_count = 1.0 / jnp.minimum(pos + 1, window).astype(F32)
        diff = (sums * inv_count - h_ref[POOL_HALO:, cols]).astype(BF16)
        y = _dot(diff, w_ref[gi].astype(BF16))
        o_ref[:, cols] = x_ref[:, cols] + y * s_ref[:, cols]


def _pool(x, gains, layer, w_groups, scales, mixer, *, seq, tm=512):
    n, d = x.shape
    _, groups, gsz, _ = w_groups.shape
    assert seq % tm == 0 and tm % POOL_HALO == 0 and groups * gsz == d
    halo_blocks = tm // POOL_HALO
    blocks = 2 * tm * d * 4 + POOL_HALO * d * 4 + groups * gsz * gsz * 4 + 2 * d * 4
    temps = 3 * (tm + POOL_HALO) * d * 4 + 2 * tm * d * 4
    return pl.pallas_call(
        functools.partial(_pool_kernel, tiles_per_batch=seq // tm),
        out_shape=jax.ShapeDtypeStruct((n, d), F32),
        grid=(n // tm,),
        in_specs=[
            pl.BlockSpec((tm, d), lambda i: (i, 0)),
            pl.BlockSpec((POOL_HALO, d), lambda i: (jnp.maximum(i * halo_blocks - 1, 0), 0)),
            pl.BlockSpec((None, 1, d), lambda i: (layer, 0, 0)),
            pl.BlockSpec((None, groups, gsz, gsz), lambda i: (mixer, 0, 0, 0)),
            pl.BlockSpec((None, 1, d), lambda i: (mixer, 0, 0)),
        ],
        out_specs=pl.BlockSpec((tm, d), lambda i: (i, 0)),
        scratch_shapes=[pltpu.VMEM((tm + POOL_HALO, d), F32)] * 3,
        compiler_params=pltpu.CompilerParams(
            dimension_semantics=("arbitrary",),
            vmem_limit_bytes=_vmem_limit(blocks, temps)),
        name="pool",
    )(x, x, gains.reshape(gains.shape[0], 1, d), w_groups, scales.reshape(scales.shape[0], 1, d))


def _log_sigmoid(z):
    return jnp.minimum(z, 0.0) - jnp.log1p(jnp.exp(-jnp.abs(z)))


def _fox_proj_kernel(x_ref, g_ref, w_ref, wf_ref, bf_ref, gq_ref, gk_ref, qkv_ref, f_ref, ks_ref,
                     h_ref, carry_ref, fmin_ref, *, tiles_per_batch, qk_tiles, tk, chunk):
    i = pl.program_id(0)
    j = pl.program_id(1)
    tm, tn = qkv_ref.shape
    blocks_per_tile = tm // tk

    def project_qk(is_q, rows, w):
        acc = _dot_nt(h_ref[rows, :], w)
        gain = jnp.where(is_q, gq_ref[...], gk_ref[...])
        post = jnp.where(is_q, LOG2E / math.sqrt(FOX_HEAD_DIM), 1.0)
        for c0 in range(0, tn, FOX_HEAD_DIM):
            y = _rms_norm(acc[:, c0:c0 + FOX_HEAD_DIM], gain) * post
            qkv_ref[rows, c0:c0 + FOX_HEAD_DIM] = y.astype(BF16)

    @pl.when(j == 0)
    def _():
        tile_in_batch = lax.rem(i, tiles_per_batch)

        @pl.when(tile_in_batch == 0)
        def _():
            carry_ref[...] = jnp.zeros_like(carry_ref)
            fmin_ref[...] = jnp.zeros_like(fmin_ref)

        for c0 in range(0, tm, chunk):
            h_ref[c0:c0 + chunk, :] = _rms_norm(x_ref[c0:c0 + chunk, :], g_ref[...]).astype(BF16)
        log_f_terms = _split_bf16(_log_sigmoid(_dot_nt(h_ref[...], wf_ref[...]) + bf_ref[...]))
        project_qk(True, slice(None), w_ref[...].astype(BF16))

        r = lax.broadcasted_iota(jnp.int32, (chunk, chunk), 0)
        c = lax.broadcasted_iota(jnp.int32, (chunk, chunk), 1)
        tri = (c <= r).astype(BF16)
        carry = carry_ref[...]
        block_max = [None] * blocks_per_tile
        block_min = [None] * blocks_per_tile
        for c0 in range(0, tm, chunk):
            prefix = sum(_dot(tri, t[c0:c0 + chunk, :]) for t in log_f_terms)
            total = prefix + carry
            carry = total[chunk - 1:chunk, :]
            f2 = total * LOG2E
            for t, term in enumerate(_split_bf16(f2)):
                f_ref[c0:c0 + chunk, t * V7X_LANES:(t + 1) * V7X_LANES] = term
            blk = c0 // tk
            hi = jnp.max(f2, axis=0, keepdims=True)
            lo = jnp.min(f2, axis=0, keepdims=True)
            block_max[blk] = hi if block_max[blk] is None else jnp.maximum(block_max[blk], hi)
            block_min[blk] = lo if block_min[blk] is None else jnp.minimum(block_min[blk], lo)
        carry_ref[...] = carry

        qk_bound = (LOG2E * math.sqrt(FOX_HEAD_DIM) * QK_BOUND_SLACK
                    * jnp.max(jnp.abs(gq_ref[...]), axis=-1, keepdims=True)
                    * jnp.max(jnp.abs(gk_ref[...]), axis=-1, keepdims=True))
        n_blocks = fmin_ref.shape[0]
        row = lax.broadcasted_iota(jnp.int32, fmin_ref.shape, 0)
        ks_ref[...] = jnp.zeros_like(ks_ref)
        for blk in range(blocks_per_tile):
            qb = tile_in_batch * blocks_per_tile + blk
            fmin_ref[pl.ds(qb, 1), :] = block_min[blk]
            gap = 2.0 * qk_bound + block_max[blk] - fmin_ref[...]
            needed = (gap > -SKIP_LOG2_MARGIN) | (row >= qb)
            ks_ref[blk:blk + 1, :] = jnp.min(jnp.where(needed, row, n_blocks), axis=0, keepdims=True)

    @pl.when((j > 0) & (j < qk_tiles))
    def _():
        w = w_ref[...].astype(BF16)
        for r0 in range(0, tm, tm // 2):
            project_qk(j < qk_tiles // 2, slice(r0, r0 + tm // 2), w)

    @pl.when(j >= qk_tiles)
    def _():
        qkv_ref[...] = _dot_nt(h_ref[...], w_ref[...].astype(BF16)).astype(BF16)


def _fox_proj(x, gains, layer, w_in_t, w_f_t, b_f, q_gains, k_gains, mixer, *, seq, tm, tk, tn, chunk=256):
    n, d = x.shape
    assert seq % tm == 0 and tm % tk == 0 and tk % chunk == 0 and d % tn == 0 and tn % FOX_HEAD_DIM == 0
    assert tm // tk <= V7X_SUBLANES
    qk_tiles = 2 * d // tn
    f_cols = F_TERMS * V7X_LANES
    blocks = tm * d * 4 + d * tn * 4 + d * V7X_LANES * 2 + tm * tn * 2 + tm * f_cols * 2
    temps = tm * d * 2 + d * tn * 2 + 3 * tm * tn * 4 + 4 * chunk * chunk * 4
    return pl.pallas_call(
        functools.partial(_fox_proj_kernel, tiles_per_batch=seq // tm, qk_tiles=qk_tiles, tk=tk, chunk=chunk),
        out_shape=(jax.ShapeDtypeStruct((n, 3 * d), BF16),
                   jax.ShapeDtypeStruct((n, f_cols), BF16),
                   jax.ShapeDtypeStruct((n // tm, V7X_SUBLANES, V7X_LANES), jnp.int32)),
        grid=(n // tm, 3 * d // tn),
        in_specs=[
            pl.BlockSpec((tm, d), lambda i, j: (i, 0)),
            pl.BlockSpec((None, 1, d), lambda i, j: (layer, 0, 0)),
            pl.BlockSpec((None, tn, d), lambda i, j: (mixer, j, 0)),
            pl.BlockSpec((None, V7X_LANES, d), lambda i, j: (mixer, 0, 0)),
            pl.BlockSpec((None, 1, V7X_LANES), lambda i, j: (mixer, 0, 0)),
            pl.BlockSpec((None, 1, FOX_HEAD_DIM), lambda i, j: (mixer, 0, 0)),
            pl.BlockSpec((None, 1, FOX_HEAD_DIM), lambda i, j: (mixer, 0, 0)),
        ],
        out_specs=(pl.BlockSpec((tm, tn), lambda i, j: (i, j)),
                   pl.BlockSpec((tm, f_cols), lambda i, j: (i, 0)),
                   pl.BlockSpec((None, V7X_SUBLANES, V7X_LANES), lambda i, j: (i, 0, 0))),
        scratch_shapes=[pltpu.VMEM((tm, d), BF16), pltpu.VMEM((1, V7X_LANES), F32),
                        pltpu.VMEM((seq // tk, V7X_LANES), F32)],
        compiler_params=pltpu.CompilerParams(
            dimension_semantics=("arbitrary", "arbitrary"),
            vmem_limit_bytes=_vmem_limit(blocks, temps)),
        name="fox_proj",
    )(x, gains.reshape(gains.shape[0], 1, d), w_in_t, w_f_t, b_f,
      q_gains.reshape(q_gains.shape[0], 1, -1), k_gains.reshape(k_gains.shape[0], 1, -1))


def _fox_attn_kernel(first_ref, q_ref, k_ref, v_ref, f_ref, o_ref, kaug_ref, qaug_ref, s_ref, cmax_ref, m_ref,
                     l_ref, acc_ref, *, tq, tk):
    head = pl.program_id(1)
    seq = k_ref.shape[0]
    nq = seq // tq
    assert tq == tk
    table_base = (pl.program_id(0) * pl.num_programs(1) + head) * nq

    r = lax.broadcasted_iota(jnp.int32, (F_TERMS * V7X_LANES, V7X_LANES), 0)
    c = lax.broadcasted_iota(jnp.int32, (F_TERMS * V7X_LANES, V7X_LANES), 1)
    sel_q = (r == c * V7X_LANES + head).astype(BF16)
    sel_k = -(r == (c - F_TERMS) * V7X_LANES + head).astype(BF16)
    lane = lax.broadcasted_iota(jnp.int32, (1, V7X_LANES), 1)
    ones_q = ((lane >= F_TERMS) & (lane < 2 * F_TERMS)).astype(F32)
    ones_k = (lane < F_TERMS).astype(F32)

    kaug_ref[:, 0:FOX_HEAD_DIM] = k_ref[...]
    kaug_ref[:, FOX_HEAD_DIM:] = (_dot(f_ref[...], sel_k) + ones_k).astype(BF16)

    def rows_of(block, n_blocks):
        return pl.ds(pl.multiple_of(block * tk, tk), n_blocks * tk)

    def build_q(qi):
        qaug_ref[:, 0:FOX_HEAD_DIM] = q_ref[rows_of(qi, 1), :]
        qaug_ref[:, FOX_HEAD_DIM:] = (_dot(f_ref[rows_of(qi, 1), :], sel_q) + ones_q).astype(BF16)

    def logits(rows):
        return _dot_nt(kaug_ref[rows, :], qaug_ref[...])

    def tail_scores(qi, with_prev):
        s = logits(rows_of(qi - 1, 2) if with_prev else rows_of(qi, 1))
        key = lax.broadcasted_iota(jnp.int32, (tk, tq), 0)
        qry = lax.broadcasted_iota(jnp.int32, (tk, tq), 1)
        diag = jnp.where(key <= qry, s[-tk:], NEG_LARGE)
        cmax = jnp.max(diag, axis=0, keepdims=True)
        if with_prev:
            cmax = jnp.maximum(cmax, jnp.max(s[:tk], axis=0, keepdims=True))
            s_ref[0:tk, :] = s[:tk]
            s_ref[tk:, :] = diag
        else:
            s_ref[0:tk, :] = diag
        cmax_ref[...] = cmax

    def online_update(s, block_max, v_rows):
        m_old = m_ref[...]
        m_new = jnp.maximum(m_old, block_max)
        alpha = jnp.exp2(m_old - m_new)
        p = jnp.exp2(s - m_new)
        l_ref[...] = alpha * l_ref[...] + jnp.sum(p, axis=0, keepdims=True)
        pv = lax.dot_general(v_ref[v_rows, :], p.astype(BF16), (((0,), (0,)), ((), ())),
                             preferred_element_type=F32)
        acc_ref[...] = alpha * acc_ref[...] + pv
        m_ref[...] = m_new

    def tail_accumulate(qi, with_prev):
        if with_prev:
            online_update(s_ref[...], cmax_ref[...], rows_of(qi - 1, 2))
        else:
            online_update(s_ref[0:tk, :], cmax_ref[...], rows_of(qi, 1))

    def start_block(qi):
        m_ref[...] = jnp.full_like(m_ref, NEG_LARGE)
        l_ref[...] = jnp.zeros_like(l_ref)
        acc_ref[...] = jnp.zeros_like(acc_ref)

        def body(kb, c):
            s = logits(rows_of(kb, 1))
            online_update(s, jnp.max(s, axis=0, keepdims=True), rows_of(kb, 1))
            return c
        lax.fori_loop(first_ref[table_base + qi], qi - 1, body, 0)

    def finish_block(qi):
        o_ref[rows_of(qi, 1), :] = (acc_ref[...] / l_ref[...]).T.astype(BF16)

    build_q(0)
    tail_scores(0, False)
    start_block(0)
    build_q(1)
    tail_accumulate(0, False)
    tail_scores(1, True)
    finish_block(0)

    def query_block(qi, carry):
        start_block(qi)
        build_q(qi + 1)
        tail_accumulate(qi, True)
        tail_scores(qi + 1, True)
        finish_block(qi)
        return carry
    lax.fori_loop(1, nq - 1, query_block, 0)

    start_block(nq - 1)
    tail_accumulate(nq - 1, True)
    finish_block(nq - 1)


def _fox_attn(first_block, qkv, f_terms, *, batch, seq, heads, tq):
    n = qkv.shape[0]
    d = heads * FOX_HEAD_DIM
    nq = seq // tq
    f_cols = f_terms.shape[1]
    assert seq % tq == 0 and nq >= 2 and qkv.shape == (n, 3 * d) and first_block.shape == (batch * heads * nq,)
    blocks = 4 * seq * FOX_HEAD_DIM * 2 + seq * f_cols * 2
    temps = (seq + tq) * 2 * FOX_HEAD_DIM * 2 + tq * FOX_HEAD_DIM * 4 + 10 * tq * tq * 4
    return pl.pallas_call(
        functools.partial(_fox_attn_kernel, tq=tq, tk=tq),
        out_shape=jax.ShapeDtypeStruct((n, d), BF16),
        grid_spec=pltpu.PrefetchScalarGridSpec(
            num_scalar_prefetch=1,
            grid=(batch, heads),
            in_specs=[
                pl.BlockSpec((seq, FOX_HEAD_DIM), lambda b, h, first: (b, h)),
                pl.BlockSpec((seq, FOX_HEAD_DIM), lambda b, h, first: (b, heads + h)),
                pl.BlockSpec((seq, FOX_HEAD_DIM), lambda b, h, first: (b, 2 * heads + h)),
                pl.BlockSpec((seq, f_cols), lambda b, h, first: (b, 0)),
            ],
            out_specs=pl.BlockSpec((seq, FOX_HEAD_DIM), lambda b, h, first: (b, h)),
            scratch_shapes=[pltpu.VMEM((seq, 2 * FOX_HEAD_DIM), BF16), pltpu.VMEM((tq, 2 * FOX_HEAD_DIM), BF16),
                            pltpu.VMEM((2 * tq, tq), F32), pltpu.VMEM((1, tq), F32), pltpu.VMEM((1, tq), F32),
                            pltpu.VMEM((1, tq), F32), pltpu.VMEM((FOX_HEAD_DIM, tq), F32)]),
        compiler_params=pltpu.CompilerParams(
            dimension_semantics=("arbitrary", "arbitrary"),
            vmem_limit_bytes=_vmem_limit(blocks, temps)),
        name="fox_attn",
    )(first_block, qkv, qkv, qkv, f_terms)


def _fox_out_kernel(o_ref, w_ref, x_ref, y_ref):
    y_ref[...] = x_ref[...] + _dot(o_ref[...], w_ref[...].astype(BF16))


def _fox_out(o, w_out, mixer, x, *, tm, tn):
    n, d = x.shape
    assert n % tm == 0 and d % tn == 0
    blocks = tm * d * 2 + d * tn * 4 + 2 * tm * tn * 4
    temps = d * tn * 2 + tm * tn * 4
    return pl.pallas_call(
        _fox_out_kernel,
        out_shape=jax.ShapeDtypeStruct((n, d), F32),
        grid=(d // tn, n // tm),
        in_specs=[
            pl.BlockSpec((tm, d), lambda j, i: (i, 0)),
            pl.BlockSpec((None, d, tn), lambda j, i: (mixer, 0, j)),
            pl.BlockSpec((tm, tn), lambda j, i: (i, j)),
        ],
        out_specs=pl.BlockSpec((tm, tn), lambda j, i: (i, j)),
        compiler_params=pltpu.CompilerParams(
            dimension_semantics=("arbitrary", "arbitrary"),
            vmem_limit_bytes=_vmem_limit(blocks, temps)),
        name="fox_out",
    )(o, w_out, x)


def _fox(x, gains, layer, w_in_t, w_f_t, b_f, q_gains, k_gains, w_out, mixer, *, batch, seq, tm, tq):
    heads = x.shape[1] // FOX_HEAD_DIM
    qkv, f_terms, first = _fox_proj(x, gains, layer, w_in_t, w_f_t, b_f, q_gains, k_gains, mixer,
                                    seq=seq, tm=tm, tk=tq, tn=min(1024, x.shape[1]))
    first = first[:, :tm // tq, :heads].reshape(batch, seq // tq, heads).transpose(0, 2, 1).reshape(-1)
    o = _fox_attn(first, qkv, f_terms, batch=batch, seq=seq, heads=heads, tq=tq)
    return _fox_out(o, w_out, mixer, x, tm=tm, tn=min(1024, x.shape[1]))


def kernel(x, ffn1_norm, ffn1_w_gate, ffn1_w_up, ffn1_w_down, mix_norm, pool_w, pool_scale, fox_w_in, fox_b_f,
           fox_q_gain, fox_k_gain, fox_w_out, ffn2_norm, ffn2_w_gate, ffn2_w_up, ffn2_w_down):
    batch, seq, d = x.shape
    depth = ffn1_norm.shape[0]
    n_mixers = 2
    heads = d // FOX_HEAD_DIM
    tm = min(1024, seq)
    tq = min(512, seq)
    assert heads <= V7X_LANES

    w_in_t = jnp.swapaxes(fox_w_in, 1, 2)
    w_f_t = jnp.pad(w_in_t[:, 3 * d:, :], ((0, 0), (0, V7X_LANES - heads), (0, 0))).astype(BF16)
    b_f = jnp.pad(fox_b_f, ((0, 0), (0, V7X_LANES - heads))).reshape(-1, 1, V7X_LANES)

    h = x.reshape(batch * seq, d)
    for i in range(depth):
        h = _ffn(h, ffn1_norm, ffn1_w_gate, ffn1_w_up, ffn1_w_down, i, tm=tm)
        j = i // n_mixers
        if i % n_mixers == 0:
            h = _pool(h, mix_norm, i, pool_w, pool_scale, j, seq=seq, tm=min(512, seq))
        else:
            h = _fox(h, mix_norm, i, w_in_t, w_f_t, b_f, fox_q_gain, fox_k_gain, fox_w_out, j,
                     batch=batch, seq=seq, tm=tm, tq=tq)
        h = _ffn(h, ffn2_norm, ffn2_w_gate, ffn2_w_up, ffn2_w_down, i, tm=tm)
    return h.reshape(batch, seq, d)
```
